```python
import math
import jax, jax.numpy as jnp
from jax import lax
import numpy as np

D_MODEL = 4096
BATCH = 32
SEQ = 256
DEPTH = 2
DEC_BATCH = 2
DEC_SEQ = 4096
PAST_LEN = 512

GRID_W = 64
D_MIX = D_MODEL
S5_WIDTH = D_MIX // 4
S5_CH = 16
S5_GROUPS = S5_WIDTH // S5_CH
S5_P = 64
GLA_WIDTH = 3 * D_MIX // 8
GLA_HEADS = 4
GLA_DK = GLA_WIDTH // (2 * GLA_HEADS)
GLA_DV = GLA_WIDTH // GLA_HEADS
GLA_RANK = 16
GLA_TAU = 16.0
HG_WIDTH = D_MIX - S5_WIDTH - GLA_WIDTH
HG_EXPAND = 128
HG_HEADS = HG_WIDTH // HG_EXPAND
HG_DV = HG_WIDTH // HG_HEADS
CHUNK = 32
D_FF = 11008
N_MOD = 9
EPS = 1e-6
F_FLOOR = 1e-30
IN_SPLITS = (S5_WIDTH, GLA_HEADS * GLA_DK, GLA_HEADS * GLA_DK, GLA_WIDTH, GLA_WIDTH, 2 * GLA_RANK, HG_WIDTH, HG_WIDTH, HG_WIDTH, HG_WIDTH, HG_WIDTH)
D_IN = S5_WIDTH + 2 * GLA_HEADS * GLA_DK + 2 * GLA_WIDTH + 2 * GLA_RANK + 5 * HG_WIDTH

kernel_name = 'hybrid_s5_gla_hgrn2_prefix_diffusion_step'


def rmsnorm(x, g):
    xf = x.astype(jnp.float32)
    y = xf * lax.rsqrt(jnp.mean(xf * xf, axis=-1, keepdims=True) + EPS)
    return (y * g.astype(jnp.float32)).astype(x.dtype)


def swiglu(h, w_gate, w_up, w_down):
    return (jax.nn.silu(h @ w_gate) * (h @ w_up)) @ w_down


def flip(t):
    return jnp.flip(t, axis=1)


def grid_to_colmajor(t):
    b, n = t.shape[:2]
    rows = n // GRID_W
    return t.reshape(b, rows, GRID_W, *t.shape[2:]).swapaxes(1, 2).reshape(t.shape)


def colmajor_to_grid(t):
    b, n = t.shape[:2]
    rows = n // GRID_W
    return t.reshape(b, GRID_W, rows, *t.shape[2:]).swapaxes(1, 2).reshape(t.shape)


def _cplx_combine(e1, e2):
    a1r, a1i, b1r, b1i = e1
    a2r, a2i, b2r, b2i = e2
    return (a2r * a1r - a2i * a1i, a2r * a1i + a2i * a1r,
            a2r * b1r - a2i * b1i + b2r, a2r * b1i + a2i * b1r + b2i)


def s5_scan(u, a_re, a_im, log_dt, b_re, b_im, c_re, c_im, h0_re, h0_im):
    dt = jnp.exp(log_dt)[:, None]
    mag = jnp.exp(a_re * dt)
    lam_re = mag * jnp.cos(a_im * dt)
    lam_im = mag * jnp.sin(a_im * dt)
    den = a_re * a_re + a_im * a_im
    z_re = ((lam_re - 1.0) * a_re + lam_im * a_im) / den
    z_im = (lam_im * a_re - (lam_re - 1.0) * a_im) / den
    bb_re = z_re[..., None] * b_re - z_im[..., None] * b_im
    bb_im = z_re[..., None] * b_im + z_im[..., None] * b_re
    bu_re = jnp.einsum('blgc,gpc->blgp', u, bb_re)
    bu_im = jnp.einsum('blgc,gpc->blgp', u, bb_im)
    bu_re = bu_re.at[:, 0].add(lam_re * h0_re - lam_im * h0_im)
    bu_im = bu_im.at[:, 0].add(lam_re * h0_im + lam_im * h0_re)
    ar = jnp.broadcast_to(lam_re, bu_re.shape)
    ai = jnp.broadcast_to(lam_im, bu_im.shape)
    _, _, x_re, x_im = lax.associative_scan(_cplx_combine, (ar, ai, bu_re, bu_im), axis=1)
    y = jnp.einsum('blgp,gcp->blgc', x_re, c_re) - jnp.einsum('blgp,gcp->blgc', x_im, c_im)
    return y, x_re[:, -1], x_im[:, -1]


def chunk_gla(q, k, v, log_g, h0):
    bsz, n, nh, _ = q.shape
    nc = n // CHUNK

    def to_chunks(t):
        return t.reshape(bsz, nc, CHUNK, nh, t.shape[-1]).transpose(1, 0, 3, 2, 4)

    causal = jnp.tril(jnp.ones((CHUNK, CHUNK), dtype=bool))[:, :, None]

    def step(s, blk):
        qc, kc, vc, gc = blk
        b = jnp.cumsum(gc, axis=2)
        diff = b[:, :, :, None, :] - b[:, :, None, :, :]
        decay = jnp.where(causal, jnp.exp(jnp.where(causal, diff, 0.0)), 0.0)
        attn = jnp.einsum('bhik,bhjk,bhijk->bhij', qc, kc, decay)
        b_last = b[:, :, -1:, :]
        o = (jnp.einsum('bhij,bhjv->bhiv', attn, vc)
             + jnp.einsum('bhik,bhkv->bhiv', qc * jnp.exp(b), s))
        s = (jnp.exp(b_last[:, :, 0, :])[..., None] * s
             + jnp.einsum('bhjk,bhjv->bhkv', kc * jnp.exp(b_last - b), vc))
        return s, o

    s_fin, o = lax.scan(step, h0, (to_chunks(q), to_chunks(k), to_chunks(v), to_chunks(log_g)))
    o = o.transpose(1, 0, 3, 2, 4).reshape(bsz, n, nh, v.shape[-1])
    return o, s_fin


def bidirectional_gla(q, k_f, k_b, v, lg_f, lg_b, h0):
    o_f, s_f = chunk_gla(q, k_f, v, lg_f, h0[:, 0])
    o_b, s_b = chunk_gla(flip(q), flip(k_b), flip(v), flip(lg_b), h0[:, 1])
    return o_f + flip(o_b), jnp.stack([s_f, s_b], axis=1)


def head_norm_gate(o, g_norm, gate):
    o = o * lax.rsqrt(jnp.mean(o * o, axis=-1, keepdims=True) + EPS) * g_norm.astype(jnp.float32)
    return o.reshape(o.shape[0], o.shape[1], -1) * jax.nn.silu(gate.astype(jnp.float32))


def s5_dir_params(lw, d):
    f32 = jnp.float32
    return tuple(lw[name][d].astype(f32) for name in ('s5_a_re', 's5_a_im', 's5_log_dt', 's5_b_re', 's5_b_im', 's5_c_re', 's5_c_im'))


def token_mixer(h, st, lw, latent):
    f32 = jnp.float32
    bsz, n = h.shape[:2]
    split_at = [int(s) for s in np.cumsum(IN_SPLITS)[:-1]]
    (u_a, q_b, k_b, v_b, g_b, lr_b, q_c, zf_c, zb_c, i_c, g_c) = jnp.split(h @ lw['w_in'], split_at, axis=-1)
    s5_re0, s5_im0, gla0, hg0 = (s.astype(f32) for s in st)

    u = u_a.astype(f32).reshape(bsz, n, S5_GROUPS, S5_CH)
    y_f, fre_f, fim_f = s5_scan(u, *s5_dir_params(lw, 0), s5_re0[:, 0], s5_im0[:, 0])
    y_b, fre_b, fim_b = s5_scan(flip(u), *s5_dir_params(lw, 1), s5_re0[:, 1], s5_im0[:, 1])
    y_a = jax.nn.gelu((y_f + flip(y_b) + lw['s5_d'].astype(f32) * u).reshape(bsz, n, S5_WIDTH))
    out_a = y_a * jax.nn.sigmoid(y_a @ lw['s5_glu_w'].astype(f32) + lw['s5_glu_b'].astype(f32))

    qb = q_b.astype(f32).reshape(bsz, n, GLA_HEADS, GLA_DK) * GLA_DK ** -0.5
    kb = k_b.astype(f32).reshape(bsz, n, GLA_HEADS, GLA_DK)
    vb = v_b.astype(f32).reshape(bsz, n, GLA_HEADS, GLA_DV)
    lr = lr_b.astype(f32)
    w2 = lw['gla_w2'].astype(f32)
    b2 = lw['gla_b2'].astype(f32)
    lg = [(jax.nn.log_sigmoid(lr[..., d * GLA_RANK:(d + 1) * GLA_RANK] @ w2[d] + b2[d]) / GLA_TAU).reshape(bsz, n, GLA_HEADS, GLA_DK) for d in range(2)]
    o_b, fin_gla = bidirectional_gla(qb, kb, kb, vb, lg[0], lg[1], gla0)
    out_b = head_norm_gate(o_b, lw['gla_norm_g'], g_b)

    qc = q_c.astype(f32).reshape(bsz, n, HG_HEADS, HG_EXPAND) * HG_EXPAND ** -0.5
    vc = i_c.astype(f32).reshape(bsz, n, HG_HEADS, HG_DV)
    lb = lw['hg_lb']
    log_f, keys = [], []
    for d, z in enumerate((zf_c, zb_c)):
        f_gate = lb[d] + (1.0 - lb[d]) * jax.nn.sigmoid(z.astype(f32))
        log_f.append(jnp.log(jnp.maximum(f_gate, F_FLOOR)).reshape(bsz, n, HG_HEADS, HG_EXPAND))
        keys.append((1.0 - f_gate).reshape(bsz, n, HG_HEADS, HG_EXPAND))
    seqs = (qc, keys[0], keys[1], vc, log_f[0], log_f[1])
    if latent:
        seqs = tuple(grid_to_colmajor(t) for t in seqs)
    o_c, fin_hg = bidirectional_gla(*seqs, hg0)
    if latent:
        o_c = colmajor_to_grid(o_c)
    out_c = head_norm_gate(o_c, lw['hg_norm_g'], g_c)

    mixed = jnp.concatenate([out_a, out_b, out_c], axis=-1).astype(h.dtype)
    new_st = (jnp.stack([fre_f, fre_b], axis=1), jnp.stack([fim_f, fim_b], axis=1), fin_gla, fin_hg)
    return mixed @ lw['w_out'], new_st


def trunk_layer(x, cond, st, lw, latent):
    m = (jax.nn.silu(cond) @ lw['ada_w'] + lw['ada_b'])[:, None, :].astype(x.dtype)
    sh1, sc1, g1, sh2, sc2, g2, sh3, sc3, g3 = jnp.split(m, N_MOD, axis=-1)
    ng = lw['norm_g']
    h = rmsnorm(x, ng[0]) * (1.0 + sc1) + sh1
    x = x + 0.5 * g1 * swiglu(h, lw['ffn1_wg'], lw['ffn1_wu'], lw['ffn1_wd'])
    h = rmsnorm(x, ng[1]) * (1.0 + sc2) + sh2
    mix, new_st = token_mixer(h, st, lw, latent)
    x = x + g2 * mix
    h = rmsnorm(x, ng[2]) * (1.0 + sc3) + sh3
    x = x + 0.5 * g3 * swiglu(h, lw['ffn2_wg'], lw['ffn2_wu'], lw['ffn2_wd'])
    return x, new_st


def setup_inputs(seed: int = 0) -> dict:
    key = jax.random.key(seed)
    ks = list(jax.random.split(key, 48))
    f32 = jnp.float32

    def nrm(shape, scale=1.0):
        return scale * jax.random.normal(ks.pop(), shape, f32)

    def gain(shape):
        return 1.0 + 0.02 * jax.random.normal(ks.pop(), shape, f32)

    G, P, CH = S5_GROUPS, S5_P, S5_CH
    inp = {}
    inp['x_prompt'] = nrm((BATCH, SEQ, D_MODEL))
    inp['x_sample'] = nrm((DEC_BATCH, DEC_SEQ, D_MODEL))
    inp['state_s5_re'] = nrm((DEC_BATCH, DEPTH, 2, G, P), 0.3)
    inp['state_s5_im'] = nrm((DEC_BATCH, DEPTH, 2, G, P), 0.3)
    inp['state_gla'] = nrm((DEC_BATCH, DEPTH, 2, GLA_HEADS, GLA_DK, GLA_DV))
    inp['state_hgrn'] = nrm((DEC_BATCH, DEPTH, 2, HG_HEADS, HG_EXPAND, HG_DV), 0.5)
    inp['c'] = nrm((DEC_BATCH, D_MODEL))
    inp['c_ctx'] = nrm((D_MODEL,))
    inp['ada_w'] = nrm((DEPTH, D_MODEL, N_MOD * D_MODEL), 0.5 * D_MODEL ** -0.5)
    inp['ada_b'] = nrm((DEPTH, N_MOD * D_MODEL), 0.02)
    inp['norm_g'] = gain((DEPTH, 3, D_MODEL))
    inp['ffn1_wg'] = nrm((DEPTH, D_MODEL, D_FF), D_MODEL ** -0.5)
    inp['ffn1_wu'] = nrm((DEPTH, D_MODEL, D_FF), D_MODEL ** -0.5)
    inp['ffn1_wd'] = nrm((DEPTH, D_FF, D_MODEL), D_FF ** -0.5)
    inp['ffn2_wg'] = nrm((DEPTH, D_MODEL, D_FF), D_MODEL ** -0.5)
    inp['ffn2_wu'] = nrm((DEPTH, D_MODEL, D_FF), D_MODEL ** -0.5)
    inp['ffn2_wd'] = nrm((DEPTH, D_FF, D_MODEL), D_FF ** -0.5)
    inp['w_in'] = nrm((DEPTH, D_MODEL, D_IN), D_MODEL ** -0.5)
    inp['w_out'] = nrm((DEPTH, D_MIX, D_MODEL), D_MIX ** -0.5)
    inp['s5_a_re'] = -0.5 + nrm((DEPTH, 2, G, P), 0.01)
    inp['s5_a_im'] = math.pi * jnp.arange(P, dtype=f32) + nrm((DEPTH, 2, G, P), 0.01)
    inp['s5_log_dt'] = jax.random.uniform(ks.pop(), (DEPTH, 2, G), f32, math.log(1e-3), math.log(1e-1))
    inp['s5_b_re'] = nrm((DEPTH, 2, G, P, CH), (2.0 * CH) ** -0.5)
    inp['s5_b_im'] = nrm((DEPTH, 2, G, P, CH), (2.0 * CH) ** -0.5)
    inp['s5_c_re'] = nrm((DEPTH, 2, G, CH, P), (2.0 * P) ** -0.5)
    inp['s5_c_im'] = nrm((DEPTH, 2, G, CH, P), (2.0 * P) ** -0.5)
    inp['s5_d'] = nrm((DEPTH, G, CH))
    inp['s5_glu_w'] = nrm((DEPTH, S5_WIDTH, S5_WIDTH), S5_WIDTH ** -0.5)
    inp['s5_glu_b'] = nrm((DEPTH, S5_WIDTH), 0.02)
    inp['gla_w2'] = nrm((DEPTH, 2, GLA_RANK, GLA_HEADS * GLA_DK), GLA_RANK ** -0.5)
    inp['gla_b2'] = nrm((DEPTH, 2, GLA_HEADS * GLA_DK), 0.02)
    inp['gla_norm_g'] = gain((DEPTH, GLA_DV))
    inp['hg_lb_raw'] = nrm((2, DEPTH, HG_WIDTH), 0.5)
    inp['hg_norm_g'] = gain((DEPTH, HG_DV))
    inp['final_norm_g'] = gain((D_MODEL,))
    return inp


def reference(x_prompt, x_sample, state_s5_re, state_s5_im, state_gla, state_hgrn, c,
              c_ctx, ada_w, ada_b, norm_g, ffn1_wg, ffn1_wu, ffn1_wd, ffn2_wg, ffn2_wu, ffn2_wd,
              w_in, w_out, s5_a_re, s5_a_im, s5_log_dt, s5_b_re, s5_b_im, s5_c_re, s5_c_im, s5_d,
              s5_glu_w, s5_glu_b, gla_w2, gla_b2, gla_norm_g, hg_lb_raw, hg_norm_g, final_norm_g):
    f32 = jnp.float32
    lb_p = jax.nn.softmax(hg_lb_raw.astype(f32), axis=1)
    hg_lb = jnp.cumsum(lb_p, axis=1) - lb_p[:, :1]

    def layer_weights(l):
        return {'ada_w': ada_w[l], 'ada_b': ada_b[l], 'norm_g': norm_g[l],
                'ffn1_wg': ffn1_wg[l], 'ffn1_wu': ffn1_wu[l], 'ffn1_wd': ffn1_wd[l],
                'ffn2_wg': ffn2_wg[l], 'ffn2_wu': ffn2_wu[l], 'ffn2_wd': ffn2_wd[l],
                'w_in': w_in[l], 'w_out': w_out[l],
                's5_a_re': s5_a_re[l], 's5_a_im': s5_a_im[l], 's5_log_dt': s5_log_dt[l],
                's5_b_re': s5_b_re[l], 's5_b_im': s5_b_im[l], 's5_c_re': s5_c_re[l], 's5_c_im': s5_c_im[l],
                's5_d': s5_d[l], 's5_glu_w': s5_glu_w[l], 's5_glu_b': s5_glu_b[l],
                'gla_w2': gla_w2[l], 'gla_b2': gla_b2[l], 'gla_norm_g': gla_norm_g[l],
                'hg_lb': hg_lb[:, l], 'hg_norm_g': hg_norm_g[l]}

    bp = x_prompt.shape[0]
    zero_st = (jnp.zeros((bp, 2, S5_GROUPS, S5_P), f32), jnp.zeros((bp, 2, S5_GROUPS, S5_P), f32),
               jnp.zeros((bp, 2, GLA_HEADS, GLA_DK, GLA_DV), f32),
               jnp.zeros((bp, 2, HG_HEADS, HG_EXPAND, HG_DV), f32))
    xc = x_prompt
    ctx_states = []
    for l in range(DEPTH):
        xc, st = trunk_layer(xc, c_ctx[None, :], zero_st, layer_weights(l), False)
        ctx_states.append(st)
    y_prompt = rmsnorm(xc, final_norm_g)
    new_s5_re = jnp.stack([s[0] for s in ctx_states], axis=1)
    new_s5_im = jnp.stack([s[1] for s in ctx_states], axis=1)
    new_gla = jnp.stack([s[2] for s in ctx_states], axis=1)
    new_hgrn = jnp.stack([s[3] for s in ctx_states], axis=1)

    xs = x_sample
    for l in range(DEPTH):
        st = (state_s5_re[:, l], state_s5_im[:, l], state_gla[:, l], state_hgrn[:, l])
        xs, _ = trunk_layer(xs, c, st, layer_weights(l), True)
    y_sample = rmsnorm(xs, final_norm_g)
    return (y_prompt, y_sample, new_s5_re, new_s5_im, new_gla, new_hgrn)
```

```python
import functools
import math

import jax
import jax.numpy as jnp
from jax import lax
from jax.experimental import pallas as pl
from jax.experimental.pallas import tpu as pltpu

F32 = jnp.float32
BF16 = jnp.bfloat16

D_MODEL = 4096
BATCH = 32
SEQ = 256
DEPTH = 2
DEC_BATCH = 2
DEC_SEQ = 4096
GRID_W = 64
S5_WIDTH = 1024
S5_CH = 16
S5_GROUPS = 64
S5_P = 64
GLA_WIDTH = 1536
GLA_HEADS = 4
GLA_DK = 192
GLA_DV = 384
GLA_RANK = 16
GLA_TAU = 16.0
HG_WIDTH = 1536
HG_EXPAND = 128
HG_HEADS = 12
HG_DV = 128
D_FF = 11008
N_MOD = 9
EPS = 1e-6
F_FLOOR = 1e-30

T_CTX = BATCH * SEQ
T_LAT = DEC_BATCH * DEC_SEQ
T_ALL = T_CTX + T_LAT
N_COND = 8
LANE = 128
MXU_W = 256
GLA_DKP = 256
S5_CB = 4
S5_GB = S5_GROUPS // S5_CB
S5_NB = S5_GB * S5_P
CHUNK = 256
VMEM_LIMIT = 56 * 1024 * 1024

TM_FFN = 512
TF_FFN = 256
TM_MM = 1024
TM_EW = 256
ROWS_EW = 64
COLS_DOWN = 1024

P1_S5U = 0
P1_GQ = 4
P1_GK = 8
P1_HQ = 12
P1_HZ = 18
P1_HI = 30
P1_HG = 36
P1_LR = 42
P1_SLOTS = 44


def _cparams(sem):
    return pltpu.CompilerParams(dimension_semantics=sem, vmem_limit_bytes=VMEM_LIMIT)


def _mod_row(i, tm):
    start = i * tm
    return jnp.where(start < T_CTX, 0, 1 + (start - T_CTX) // DEC_SEQ)


def _mod_spec(layer, j, tm):
    return pl.BlockSpec((None, None, None, 1, D_MODEL),
                        lambda i, *_: (layer, _mod_row(i, tm), j, 0, 0))


def _sigmoid(x):
    return 1.0 / (1.0 + jnp.exp(-x))


def _silu(x):
    return x * _sigmoid(x)


def _norm_mod(x, g, sc, sh):
    ms = jnp.mean(x * x, axis=-1, keepdims=True)
    y = x * lax.rsqrt(ms + EPS) * g
    return y * (1.0 + sc) + sh


def _ada_kernel(c_ref, w_ref, b_ref, o_ref):
    s = _silu(c_ref[...]).astype(BF16)
    o_ref[...] = jnp.dot(s, w_ref[...].astype(BF16), preferred_element_type=F32) + b_ref[...]


def _ada(cond8, ada_w, ada_b):
    tn = 512
    n = N_MOD * D_MODEL
    return pl.pallas_call(
        _ada_kernel,
        grid=(DEPTH, n // tn),
        in_specs=[
            pl.BlockSpec((N_COND, D_MODEL), lambda l, j: (0, 0)),
            pl.BlockSpec((None, D_MODEL, tn), lambda l, j: (l, 0, j)),
            pl.BlockSpec((None, 1, tn), lambda l, j: (l, 0, j)),
        ],
        out_specs=pl.BlockSpec((None, N_COND, tn), lambda l, j: (l, 0, j)),
        out_shape=jax.ShapeDtypeStruct((DEPTH, N_COND, n), F32),
        compiler_params=_cparams(("parallel", "parallel")),
        name="ada",
    )(cond8, ada_w, ada_b.reshape(DEPTH, 1, n))


def _lb_kernel(raw_ref, o_ref):
    rows = [raw_ref[l] for l in range(DEPTH)]
    mx = functools.reduce(jnp.maximum, rows)
    es = [jnp.exp(r - mx) for r in rows]
    den = functools.reduce(lambda a, b: a + b, es)
    ps = [e / den for e in es]
    acc = ps[0]
    o_ref[0] = acc - ps[0]
    for l in range(1, DEPTH):
        acc = acc + ps[l]
        o_ref[l] = acc - ps[0]


def _hg_lb(raw):
    return pl.pallas_call(
        _lb_kernel,
        out_shape=jax.ShapeDtypeStruct(raw.shape, F32),
        name="hg_lb",
    )(raw)


def _ffn_kernel(x_ref, g_ref, sh_ref, sc_ref, gt_ref, wg_ref, wu_ref, wd_ref, o_ref, h_ref):
    @pl.when(pl.program_id(1) == 0)
    def _():
        def rows(r, carry):
            sl = pl.ds(pl.multiple_of(r * ROWS_EW, ROWS_EW), ROWS_EW)
            x = x_ref[sl, :]
            h_ref[sl, :] = _norm_mod(x, g_ref[...], sc_ref[...], sh_ref[...]).astype(BF16)
            o_ref[sl, :] = x
            return carry

        lax.fori_loop(0, x_ref.shape[0] // ROWS_EW, rows, 0)

    h = h_ref[...]
    a = jnp.dot(h, wg_ref[...], preferred_element_type=F32)
    u = jnp.dot(h, wu_ref[...], preferred_element_type=F32)
    act = (_silu(a) * u).astype(BF16)
    gate = 0.5 * gt_ref[...]
    for n in range(D_MODEL // COLS_DOWN):
        sl = slice(n * COLS_DOWN, (n + 1) * COLS_DOWN)
        o_ref[:, sl] += gate[:, sl] * jnp.dot(act, wd_ref[:, sl], preferred_element_type=F32)


def _ffn(x, mod5, gain, layer, jbase, wg, wu, wd):
    tm, tf = TM_FFN, TF_FFN
    return pl.pallas_call(
        _ffn_kernel,
        grid=(T_ALL // tm, D_FF // tf),
        in_specs=[
            pl.BlockSpec((tm, D_MODEL), lambda i, f: (i, 0)),
            pl.BlockSpec((1, D_MODEL), lambda i, f: (0, 0)),
            _mod_spec(layer, jbase, tm),
            _mod_spec(layer, jbase + 1, tm),
            _mod_spec(layer, jbase + 2, tm),
            pl.BlockSpec((D_MODEL, tf), lambda i, f: (0, f)),
            pl.BlockSpec((D_MODEL, tf), lambda i, f: (0, f)),
            pl.BlockSpec((tf, D_MODEL), lambda i, f: (f, 0)),
        ],
        out_specs=pl.BlockSpec((tm, D_MODEL), lambda i, f: (i, 0)),
        out_shape=jax.ShapeDtypeStruct((T_ALL, D_MODEL), F32),
        scratch_shapes=[pltpu.VMEM((tm, D_MODEL), BF16)],
        compiler_params=_cparams(("parallel", "arbitrary")),
        name="ffn",
    )(x, gain.reshape(1, D_MODEL), mod5, mod5, mod5, wg, wu, wd)


def _normmod_kernel(x_ref, g_ref, sh_ref, sc_ref, o_ref):
    o_ref[...] = _norm_mod(x_ref[...], g_ref[...], sc_ref[...], sh_ref[...]).astype(BF16)


def _normmod(x, mod5, gain, layer, jbase):
    tm = TM_EW
    return pl.pallas_call(
        _normmod_kernel,
        grid=(T_ALL // tm,),
        in_specs=[
            pl.BlockSpec((tm, D_MODEL), lambda i: (i, 0)),
            pl.BlockSpec((1, D_MODEL), lambda i: (0, 0)),
            _mod_spec(layer, jbase, tm),
            _mod_spec(layer, jbase + 1, tm),
        ],
        out_specs=pl.BlockSpec((tm, D_MODEL), lambda i: (i, 0)),
        out_shape=jax.ShapeDtypeStruct((T_ALL, D_MODEL), BF16),
        compiler_params=_cparams(("parallel",)),
        name="normmod",
    )(x, gain.reshape(1, D_MODEL), mod5, mod5)


def _rmsnorm_kernel(x_ref, g_ref, o_ref):
    x = x_ref[...]
    ms = jnp.mean(x * x, axis=-1, keepdims=True)
    o_ref[...] = x * lax.rsqrt(ms + EPS) * g_ref[...]


def _final_norm(x, gain):
    tm = TM_EW
    return pl.pallas_call(
        _rmsnorm_kernel,
        grid=(T_ALL // tm,),
        in_specs=[pl.BlockSpec((tm, D_MODEL), lambda i: (i, 0)),
                  pl.BlockSpec((1, D_MODEL), lambda i: (0, 0))],
        out_specs=pl.BlockSpec((tm, D_MODEL), lambda i: (i, 0)),
        out_shape=jax.ShapeDtypeStruct((T_ALL, D_MODEL), F32),
        compiler_params=_cparams(("parallel",)),
        name="final_norm",
    )(x, gain.reshape(1, D_MODEL))


def _proj_kernel(a_ref, w_ref, o_ref, *, g, tw):
    acc = jnp.dot(a_ref[...], w_ref[...], preferred_element_type=F32)
    for s in range(g):
        o_ref[s] = acc[:, s * tw:(s + 1) * tw]


def _proj(h, w, tw, g):
    n = w.shape[1]
    tm = TM_MM
    return pl.pallas_call(
        functools.partial(_proj_kernel, g=g, tw=tw),
        grid=(T_ALL // tm, n // (g * tw)),
        in_specs=[pl.BlockSpec((tm, D_MODEL), lambda i, j: (i, 0)),
                  pl.BlockSpec((D_MODEL, g * tw), lambda i, j: (0, j))],
        out_specs=pl.BlockSpec((g, tm, tw), lambda i, j: (j, i, 0)),
        out_shape=jax.ShapeDtypeStruct((n // tw, T_ALL, tw), F32),
        compiler_params=_cparams(("parallel", "arbitrary")),
        name="proj",
    )(h, w)


def _s5disc_kernel(are_ref, aim_ref, ldt_ref, bre_ref, bim_ref, lre_ref, lim_ref, bbre_ref, bbim_ref):
    a_re = are_ref[...]
    a_im = aim_ref[...]
    dt = jnp.exp(ldt_ref[...])
    mag = jnp.exp(a_re * dt)
    lam_re = mag * jnp.cos(a_im * dt)
    lam_im = mag * jnp.sin(a_im * dt)
    den = a_re * a_re + a_im * a_im
    z_re = ((lam_re - 1.0) * a_re + lam_im * a_im) / den
    z_im = (lam_im * a_re - (lam_re - 1.0) * a_im) / den
    b_re = bre_ref[...]
    b_im = bim_ref[...]
    lre_ref[...] = lam_re
    lim_ref[...] = lam_im
    bbre_ref[...] = z_re * b_re - z_im * b_im
    bbim_ref[...] = z_re * b_im + z_im * b_re


def _s5_discretise(a_re, a_im, log_dt, b_re, b_im):
    shp = (2, S5_GROUPS, S5_CH, S5_P)
    n = 2 * S5_GROUPS * S5_CH
    rep = lambda t: jnp.broadcast_to(t[:, :, None, :], shp).reshape(n, S5_P)
    ldt = jnp.broadcast_to(log_dt[:, :, None, None], shp).reshape(n, S5_P)
    tr = lambda t: t.swapaxes(-1, -2).reshape(n, S5_P)
    lam_re, lam_im, bb_re, bb_im = pl.pallas_call(
        _s5disc_kernel,
        out_shape=[jax.ShapeDtypeStruct((n, S5_P), F32)] * 4,
        name="s5_disc",
    )(rep(a_re), rep(a_im), ldt, tr(b_re), tr(b_im))
    return (lam_re.reshape(shp)[:, :, 0], lam_im.reshape(shp)[:, :, 0],
            bb_re.reshape(shp), bb_im.reshape(shp))


def _s5_kernel(u_ref, bbd_ref, cbd_ref, lam_ref, h0_ref, y_ref, fin_ref, st_ref, bu_ref, x_ref, *, tc, rev):
    @pl.when(pl.program_id(2) == 0)
    def _():
        st_ref[...] = h0_ref[...]

    bu_ref[...] = jnp.dot(u_ref[...].astype(BF16), bbd_ref[...], preferred_element_type=F32)
    lam_re = lam_ref[:, :S5_NB]
    lam_im = lam_ref[:, S5_NB:]

    def step(t, carry):
        xr, xi = carry
        tt = tc - 1 - t if rev else t
        br = bu_ref[pl.ds(tt, 1), :S5_NB]
        bi = bu_ref[pl.ds(tt, 1), S5_NB:]
        nr = lam_re * xr - lam_im * xi + br
        ni = lam_re * xi + lam_im * xr + bi
        x_ref[pl.ds(tt, 1), :S5_NB] = nr
        x_ref[pl.ds(tt, 1), S5_NB:] = ni
        return nr, ni

    xr, xi = lax.fori_loop(0, tc, step, (st_ref[:, :S5_NB], st_ref[:, S5_NB:]), unroll=8)
    st_ref[:, :S5_NB] = xr
    st_ref[:, S5_NB:] = xi
    fin_ref[:, :S5_NB] = xr
    fin_ref[:, S5_NB:] = xi
    y_ref[...] = jnp.dot(x_ref[...].astype(BF16), cbd_ref[...], preferred_element_type=F32)


def _s5_scan(p1, bbd, cbd, lam, h0, d, nb, seq, row0, tc):
    nc = seq // tc
    rev = d == 1
    blk0 = row0 // tc

    def rowblk(b, c):
        return b * nc + (nc - 1 - c if rev else c)

    return pl.pallas_call(
        functools.partial(_s5_kernel, tc=tc, rev=rev),
        grid=(nb, S5_CB, nc),
        in_specs=[
            pl.BlockSpec((None, tc, MXU_W), lambda b, cb, c: (P1_S5U + cb, blk0 + rowblk(b, c), 0)),
            pl.BlockSpec((None, None, MXU_W, 2 * S5_NB), lambda b, cb, c: (d, cb, 0, 0)),
            pl.BlockSpec((None, None, 2 * S5_NB, MXU_W), lambda b, cb, c: (d, cb, 0, 0)),
            pl.BlockSpec((None, None, 1, 2 * S5_NB), lambda b, cb, c: (d, cb, 0, 0)),
            pl.BlockSpec((None, None, None, 1, 2 * S5_NB), lambda b, cb, c: (b, d, cb, 0, 0)),
        ],
        out_specs=[
            pl.BlockSpec((None, tc, MXU_W), lambda b, cb, c: (cb, rowblk(b, c), 0)),
            pl.BlockSpec((None, None, 1, 2 * S5_NB), lambda b, cb, c: (b, cb, 0, 0)),
        ],
        out_shape=[jax.ShapeDtypeStruct((S5_CB, nb * seq, MXU_W), F32),
                   jax.ShapeDtypeStruct((nb, S5_CB, 1, 2 * S5_NB), F32)],
        scratch_shapes=[pltpu.VMEM((1, 2 * S5_NB), F32),
                        pltpu.VMEM((tc, 2 * S5_NB), F32),
                        pltpu.VMEM((tc, 2 * S5_NB), F32)],
        compiler_params=_cparams(("parallel", "parallel", "arbitrary")),
        name="s5_scan",
    )(p1, bbd, cbd, lam, h0)


def _gelu_tanh(x):
    return 0.5 * x * (1.0 + jnp.tanh(math.sqrt(2.0 / math.pi) * (x + 0.044715 * x * x * x)))


def _s5post_kernel(yf_ref, yb_ref, u_ref, d_ref, w_ref, b_ref, o_ref):
    ya = []
    for cb in range(S5_CB):
        ya.append(_gelu_tanh(yf_ref[cb] + yb_ref[cb] + d_ref[cb] * u_ref[cb]))
    z = b_ref[...]
    for cb in range(S5_CB):
        z = z + jnp.dot(ya[cb].astype(BF16), w_ref[cb * MXU_W:(cb + 1) * MXU_W, :],
                        preferred_element_type=F32)
    for cb in range(S5_CB):
        sl = slice(cb * MXU_W, (cb + 1) * MXU_W)
        o_ref[:, sl] = (ya[cb] * _sigmoid(z[:, sl])).astype(BF16)


def _s5_post(yf, yb, p1, skip, glu_w, glu_b):
    tm = TM_EW
    return pl.pallas_call(
        _s5post_kernel,
        grid=(T_ALL // tm,),
        in_specs=[
            pl.BlockSpec((S5_CB, tm, MXU_W), lambda i: (0, i, 0)),
            pl.BlockSpec((S5_CB, tm, MXU_W), lambda i: (0, i, 0)),
            pl.BlockSpec((S5_CB, tm, MXU_W), lambda i: (0, i, 0)),
            pl.BlockSpec((S5_CB, 1, MXU_W), lambda i: (0, 0, 0)),
            pl.BlockSpec((S5_WIDTH, S5_WIDTH), lambda i: (0, 0)),
            pl.BlockSpec((1, S5_WIDTH), lambda i: (0, 0)),
        ],
        out_specs=pl.BlockSpec((tm, S5_WIDTH), lambda i: (i, 0)),
        out_shape=jax.ShapeDtypeStruct((T_ALL, S5_WIDTH), BF16),
        compiler_params=_cparams(("parallel",)),
        name="s5_post",
    )(yf, yb, p1, skip, glu_w, glu_b)


def _gla_chunk(q, k, v, lg, st, rev):
    c, kk = q.shape
    rowk = lax.broadcasted_iota(jnp.int32, (c, kk), 0)
    ri = lax.broadcasted_iota(jnp.int32, (c, c), 0)
    ci = lax.broadcasted_iota(jnp.int32, (c, c), 1)
    nt = (((1,), (1,)), ((), ()))
    p = lg
    r = jnp.zeros_like(lg)
    a = jnp.where(ri == ci, jnp.sum(q * k, axis=-1, keepdims=True), 0.0)
    for lvl in range(int(math.log2(c))):
        m = 1 << lvl
        hi = (rowk & m) != 0
        later = jnp.logical_not(hi) if rev else hi
        qs = jnp.where(later, q * jnp.exp(p), 0.0).astype(BF16)
        ks = jnp.where(later, 0.0, k * jnp.exp(r)).astype(BF16)
        al = lax.dot_general(qs, ks, nt, preferred_element_type=F32)
        a = a + jnp.where((ri >> (lvl + 1)) == (ci >> (lvl + 1)), al, 0.0)
        tot = p + r
        t_prev = pltpu.roll(tot, m, 0)
        t_next = pltpu.roll(tot, c - m, 0)
        if rev:
            p = p + jnp.where(hi, 0.0, t_next)
            r = r + jnp.where(hi, t_prev, 0.0)
        else:
            p = p + jnp.where(hi, t_prev, 0.0)
            r = r + jnp.where(hi, 0.0, t_next)
    qe = (q * jnp.exp(p)).astype(BF16)
    ke = (k * jnp.exp(r)).astype(BF16)
    vb = v.astype(BF16)
    o = jnp.dot(a.astype(BF16), vb, preferred_element_type=F32)
    o = o + lax.dot_general(qe, st.astype(BF16), nt, preferred_element_type=F32)
    decay = jnp.exp((p + r)[0:1, :])
    st_new = st * decay + lax.dot_general(vb, ke, (((0,), (0,)), ((), ())), preferred_element_type=F32)
    return o, st_new


def _log_sigmoid(z):
    return jnp.minimum(z, 0.0) - jnp.log(1.0 + jnp.exp(-jnp.abs(z)))


def _gla_kernel(q_ref, k_ref, v_ref, lr_ref, w2_ref, b2_ref, h0_ref, o_ref, fin_ref, st_ref, *, rev):
    @pl.when(pl.program_id(1) == 0)
    def _():
        st_ref[...] = h0_ref[...]

    lr = lr_ref[...].astype(BF16)

    def head(h, carry):
        z = jnp.dot(lr, w2_ref[h], preferred_element_type=F32) + b2_ref[h]
        lg = _log_sigmoid(z) * (1.0 / GLA_TAU)
        q = q_ref[h] * (GLA_DK ** -0.5)
        o, st = _gla_chunk(q, k_ref[h], v_ref[h], lg, st_ref[h], rev)
        o_ref[h] = o
        st_ref[h] = st
        fin_ref[h] = st
        return carry

    lax.fori_loop(0, GLA_HEADS, head, 0)


def _gla_scan(p1, p2, w2p, b2p, h0t, d, nb, seq, row0):
    c = CHUNK
    nc = seq // c
    rev = d == 1
    blk0 = row0 // c

    def rowblk(b, ch):
        return b * nc + (nc - 1 - ch if rev else ch)

    hq = GLA_HEADS
    return pl.pallas_call(
        functools.partial(_gla_kernel, rev=rev),
        grid=(nb, nc),
        in_specs=[
            pl.BlockSpec((hq, c, GLA_DKP), lambda b, ch: (P1_GQ // hq, blk0 + rowblk(b, ch), 0)),
            pl.BlockSpec((hq, c, GLA_DKP), lambda b, ch: (P1_GK // hq, blk0 + rowblk(b, ch), 0)),
            pl.BlockSpec((hq, c, GLA_DV), lambda b, ch: (0, blk0 + rowblk(b, ch), 0)),
            pl.BlockSpec((None, c, MXU_W), lambda b, ch: (P1_LR, blk0 + rowblk(b, ch), 0)),
            pl.BlockSpec((None, hq, MXU_W, GLA_DKP), lambda b, ch: (d, 0, 0, 0)),
            pl.BlockSpec((None, hq, 1, GLA_DKP), lambda b, ch: (d, 0, 0, 0)),
            pl.BlockSpec((None, None, hq, GLA_DV, GLA_DKP), lambda b, ch: (b, d, 0, 0, 0)),
        ],
        out_specs=[
            pl.BlockSpec((hq, c, GLA_DV), lambda b, ch: (0, rowblk(b, ch), 0)),
            pl.BlockSpec((None, hq, GLA_DV, GLA_DKP), lambda b, ch: (b, 0, 0, 0)),
        ],
        out_shape=[jax.ShapeDtypeStruct((hq, nb * seq, GLA_DV), F32),
                   jax.ShapeDtypeStruct((nb, hq, GLA_DV, GLA_DKP), F32)],
        scratch_shapes=[pltpu.VMEM((hq, GLA_DV, GLA_DKP), F32)],
        compiler_params=_cparams(("parallel", "arbitrary")),
        name="gla_scan",
    )(p1, p1, p2, p1, w2p, b2p, h0t)


def _hg_kernel(q_ref, z_ref, v_ref, lb_ref, h0_ref, o_ref, fin_ref, st_ref, *, rev, colmajor):
    @pl.when(pl.program_id(1) == 0)
    def _():
        st_ref[...] = h0_ref[...]

    kk = HG_EXPAND
    ncol = CHUNK // GRID_W

    def load(ref, pr):
        if not colmajor:
            return ref[pr]
        return jnp.concatenate([ref[pr, :, j * MXU_W:(j + 1) * MXU_W] for j in range(ncol)], axis=0)

    def pair(pr, carry):
        q2 = load(q_ref, pr) * (HG_EXPAND ** -0.5)
        z2 = load(z_ref, pr)
        v2 = load(v_ref, pr)
        lb = lb_ref[pr]
        f = lb + (1.0 - lb) * _sigmoid(z2)
        lg2 = jnp.log(jnp.maximum(f, F_FLOOR))
        k2 = 1.0 - f
        outs = []
        for s in range(2):
            sl = slice(s * kk, (s + 1) * kk)
            h = 2 * pr + s
            o, st = _gla_chunk(q2[:, sl], k2[:, sl], v2[:, sl], lg2[:, sl], st_ref[h], rev)
            st_ref[h] = st
            fin_ref[h] = st
            outs.append(o)
        o2 = jnp.concatenate(outs, axis=1)
        if not colmajor:
            o_ref[pr] = o2
        else:
            for j in range(ncol):
                o_ref[pr, :, j * MXU_W:(j + 1) * MXU_W] = o2[j * GRID_W:(j + 1) * GRID_W, :]
        return carry

    lax.fori_loop(0, HG_HEADS // 2, pair, 0)


def _hg_scan(p1, lbp, h0t, d, nb, seq, row0, colmajor):
    c = CHUNK
    nc = seq // c
    rev = d == 1
    np_ = HG_HEADS // 2

    def chunk(ch):
        return nc - 1 - ch if rev else ch

    if colmajor:
        rows = seq // GRID_W
        wblk = (c // GRID_W) * MXU_W
        src = p1.reshape(P1_SLOTS, T_ALL // seq, rows, GRID_W * MXU_W)
        seq0 = row0 // seq
        ispec = lambda slot: pl.BlockSpec((np_, None, rows, wblk),
                                          lambda b, ch: (slot // np_, seq0 + b, 0, chunk(ch)))
        ospec = pl.BlockSpec((np_, None, rows, wblk), lambda b, ch: (0, b, 0, chunk(ch)))
        oshape = jax.ShapeDtypeStruct((np_, nb, rows, GRID_W * MXU_W), F32)
    else:
        src = p1
        blk0 = row0 // c
        ispec = lambda slot: pl.BlockSpec((np_, c, MXU_W),
                                          lambda b, ch: (slot // np_, blk0 + b * nc + chunk(ch), 0))
        ospec = pl.BlockSpec((np_, c, MXU_W), lambda b, ch: (0, b * nc + chunk(ch), 0))
        oshape = jax.ShapeDtypeStruct((np_, nb * seq, MXU_W), F32)

    o, fin = pl.pallas_call(
        functools.partial(_hg_kernel, rev=rev, colmajor=colmajor),
        grid=(nb, nc),
        in_specs=[
            ispec(P1_HQ),
            ispec(P1_HZ + d * np_),
            ispec(P1_HI),
            pl.BlockSpec((None, np_, 1, MXU_W), lambda b, ch: (d, 0, 0, 0)),
            pl.BlockSpec((None, None, HG_HEADS, HG_DV, HG_EXPAND), lambda b, ch: (b, d, 0, 0, 0)),
        ],
        out_specs=[
            ospec,
            pl.BlockSpec((None, HG_HEADS, HG_DV, HG_EXPAND), lambda b, ch: (b, 0, 0, 0)),
        ],
        out_shape=[oshape, jax.ShapeDtypeStruct((nb, HG_HEADS, HG_DV, HG_EXPAND), F32)],
        scratch_shapes=[pltpu.VMEM((HG_HEADS, HG_DV, HG_EXPAND), F32)],
        compiler_params=_cparams(("parallel", "arbitrary")),
        name="hg_scan",
    )(src, src, src, lbp, h0t)
    return o.reshape(np_, nb * seq, MXU_W), fin


def _headpost_kernel(of_ref, ob_ref, gate_ref, gn_ref, out_ref, *, hp, v):
    gn = gn_ref[...]
    for s in range(hp):
        sl = slice(s * v, (s + 1) * v)
        o = of_ref[:, sl] + ob_ref[:, sl]
        y = o * lax.rsqrt(jnp.mean(o * o, axis=-1, keepdims=True) + EPS) * gn
        out_ref[:, sl] = (y * _silu(gate_ref[:, sl])).astype(BF16)


def _head_post(o_f, o_b, gate_src, gate_slot0, gain, hp, v):
    tm = TM_EW
    nh, _, w = o_f.shape
    return pl.pallas_call(
        functools.partial(_headpost_kernel, hp=hp, v=v),
        grid=(T_ALL // tm, nh),
        in_specs=[
            pl.BlockSpec((None, tm, w), lambda i, j: (j, i, 0)),
            pl.BlockSpec((None, tm, w), lambda i, j: (j, i, 0)),
            pl.BlockSpec((None, tm, w), lambda i, j: (gate_slot0 + j, i, 0)),
            pl.BlockSpec((1, v), lambda i, j: (0, 0)),
        ],
        out_specs=pl.BlockSpec((tm, w), lambda i, j: (i, j)),
        out_shape=jax.ShapeDtypeStruct((T_ALL, nh * w), BF16),
        compiler_params=_cparams(("parallel", "parallel")),
        name="head_post",
    )(o_f, o_b, gate_src, gain.reshape(1, v))


def _wout_kernel(a_ref, b_ref, c_ref, wa_ref, wb_ref, wc_ref, x_ref, gt_ref, o_ref):
    acc = jnp.dot(a_ref[...], wa_ref[...], preferred_element_type=F32)
    acc = acc + jnp.dot(b_ref[...], wb_ref[...], preferred_element_type=F32)
    acc = acc + jnp.dot(c_ref[...], wc_ref[...], preferred_element_type=F32)
    o_ref[...] = x_ref[...] + gt_ref[...] * acc


def _wout(ma, mb, mc, wa, wb, wc, x, mod5, layer, jgate):
    tm, tn = TM_FFN, 1024
    return pl.pallas_call(
        _wout_kernel,
        grid=(T_ALL // tm, D_MODEL // tn),
        in_specs=[
            pl.BlockSpec((tm, S5_WIDTH), lambda i, j: (i, 0)),
            pl.BlockSpec((tm, GLA_WIDTH), lambda i, j: (i, 0)),
            pl.BlockSpec((tm, HG_WIDTH), lambda i, j: (i, 0)),
            pl.BlockSpec((S5_WIDTH, tn), lambda i, j: (0, j)),
            pl.BlockSpec((GLA_WIDTH, tn), lambda i, j: (0, j)),
            pl.BlockSpec((HG_WIDTH, tn), lambda i, j: (0, j)),
            pl.BlockSpec((tm, tn), lambda i, j: (i, j)),
            pl.BlockSpec((None, None, None, 1, tn), lambda i, j: (layer, _mod_row(i, tm), jgate, 0, j)),
        ],
        out_specs=pl.BlockSpec((tm, tn), lambda i, j: (i, j)),
        out_shape=jax.ShapeDtypeStruct((T_ALL, D_MODEL), F32),
        compiler_params=_cparams(("parallel", "arbitrary")),
        name="wout",
    )(ma, mb, mc, wa, wb, wc, x, mod5)


def _pad_heads(w, nh, dk, dkp):
    return jnp.pad(w.reshape(w.shape[0], nh, dk), ((0, 0), (0, 0), (0, dkp - dk))).reshape(w.shape[0], nh * dkp)


def _split_w_in(w):
    offs = [0]
    for s in (S5_WIDTH, GLA_HEADS * GLA_DK, GLA_HEADS * GLA_DK, GLA_WIDTH, GLA_WIDTH, 2 * GLA_RANK,
              HG_WIDTH, HG_WIDTH, HG_WIDTH, HG_WIDTH, HG_WIDTH):
        offs.append(offs[-1] + s)
    (u_a, q_b, k_b, v_b, g_b, lr_b, q_c, zf_c, zb_c, i_c, g_c) = [w[:, offs[i]:offs[i + 1]] for i in range(11)]
    w1 = jnp.concatenate([
        u_a,
        _pad_heads(q_b, GLA_HEADS, GLA_DK, GLA_DKP),
        _pad_heads(k_b, GLA_HEADS, GLA_DK, GLA_DKP),
        q_c, zf_c, zb_c, i_c, g_c,
        jnp.pad(lr_b, ((0, 0), (0, 2 * MXU_W - 2 * GLA_RANK))),
    ], axis=1).astype(BF16)
    w2 = jnp.concatenate([v_b, g_b], axis=1).astype(BF16)
    return w1, w2


def _s5_blockdiag(bb_re, bb_im, c_re, c_im):
    eye = jnp.eye(S5_GB, dtype=F32)

    def bmat(bb):
        t = bb.reshape(2, S5_CB, S5_GB, S5_CH, S5_P)
        return jnp.einsum('dbgcp,gh->dbgchp', t, eye).reshape(2, S5_CB, S5_GB * S5_CH, S5_NB)

    def cmat(cc):
        t = cc.reshape(2, S5_CB, S5_GB, S5_CH, S5_P)
        return jnp.einsum('dbgcp,gh->dbgphc', t, eye).reshape(2, S5_CB, S5_NB, S5_GB * S5_CH)

    bbd = jnp.concatenate([bmat(bb_re), bmat(bb_im)], axis=3).astype(BF16)
    cbd = jnp.concatenate([cmat(c_re), -cmat(c_im)], axis=2).astype(BF16)
    return bbd, cbd


def _s5_state_in(re, im):
    b = re.shape[0]
    f = lambda t: t.reshape(b, 2, S5_CB, 1, S5_NB)
    return jnp.concatenate([f(re), f(im)], axis=-1)


def _s5_state_out(fin):
    b = fin.shape[0]
    re = fin[..., :S5_NB].reshape(b, S5_GROUPS, S5_P)
    im = fin[..., S5_NB:].reshape(b, S5_GROUPS, S5_P)
    return re, im


def kernel(x_prompt, x_sample, state_s5_re, state_s5_im, state_gla, state_hgrn, c, c_ctx, ada_w, ada_b, norm_g, ffn1_wg, ffn1_wu, ffn1_wd, ffn2_wg, ffn2_wu, ffn2_wd, w_in, w_out, s5_a_re, s5_a_im, s5_log_dt, s5_b_re, s5_b_im, s5_c_re, s5_c_im, s5_d, s5_glu_w, s5_glu_b, gla_w2, gla_b2, gla_norm_g, hg_lb_raw, hg_norm_g, final_norm_g):
    x = jnp.concatenate([x_prompt.reshape(T_CTX, D_MODEL), x_sample.reshape(T_LAT, D_MODEL)], axis=0)
    cond8 = jnp.zeros((N_COND, D_MODEL), F32).at[0].set(c_ctx).at[1:1 + DEC_BATCH].set(c)
    mod5 = _ada(cond8, ada_w, ada_b).reshape(DEPTH, N_COND, N_MOD, 1, D_MODEL)
    hg_lb = _hg_lb(hg_lb_raw.astype(F32).swapaxes(0, 1))

    new_re, new_im, new_gla, new_hg = [], [], [], []
    for l in range(DEPTH):
        x = _ffn(x, mod5, norm_g[l, 0], l, 0,
                 ffn1_wg[l].astype(BF16), ffn1_wu[l].astype(BF16), ffn1_wd[l].astype(BF16))

        h = _normmod(x, mod5, norm_g[l, 1], l, 3)
        w1, w2 = _split_w_in(w_in[l])
        p1 = _proj(h, w1, MXU_W, 4)
        p2 = _proj(h, w2, GLA_DV, 2)

        lam_re, lam_im, bb_re, bb_im = _s5_discretise(s5_a_re[l], s5_a_im[l], s5_log_dt[l], s5_b_re[l], s5_b_im[l])
        bbd, cbd = _s5_blockdiag(bb_re, bb_im, s5_c_re[l], s5_c_im[l])
        lam = jnp.concatenate([lam_re.reshape(2, S5_CB, 1, S5_NB), lam_im.reshape(2, S5_CB, 1, S5_NB)], axis=-1)
        h0_ctx = jnp.zeros((BATCH, 2, S5_CB, 1, 2 * S5_NB), F32)
        h0_lat = _s5_state_in(state_s5_re[:, l], state_s5_im[:, l])
        ys, fins = [], []
        for d in range(2):
            y_c, fin_c = _s5_scan(p1, bbd, cbd, lam, h0_ctx, d, BATCH, SEQ, 0, SEQ)
            y_l, _ = _s5_scan(p1, bbd, cbd, lam, h0_lat, d, DEC_BATCH, DEC_SEQ, T_CTX, 512)
            ys.append(jnp.concatenate([y_c, y_l], axis=1))
            fins.append(_s5_state_out(fin_c))
        new_re.append(jnp.stack([fins[0][0], fins[1][0]], axis=1))
        new_im.append(jnp.stack([fins[0][1], fins[1][1]], axis=1))
        out_a = _s5_post(ys[0], ys[1], p1, s5_d[l].reshape(S5_CB, 1, MXU_W),
                         s5_glu_w[l].astype(BF16), s5_glu_b[l].reshape(1, S5_WIDTH))

        w2p = jnp.zeros((2, MXU_W, GLA_HEADS, GLA_DKP), F32)
        for d in range(2):
            w2p = w2p.at[d, d * GLA_RANK:(d + 1) * GLA_RANK, :, :GLA_DK].set(
                gla_w2[l, d].reshape(GLA_RANK, GLA_HEADS, GLA_DK))
        w2p = w2p.transpose(0, 2, 1, 3).astype(BF16)
        b2p = jnp.pad(gla_b2[l].reshape(2, GLA_HEADS, 1, GLA_DK), ((0, 0), (0, 0), (0, 0), (0, GLA_DKP - GLA_DK)))
        kpad = ((0, 0),) * 4 + ((0, GLA_DKP - GLA_DK),)
        g0_ctx = jnp.zeros((BATCH, 2, GLA_HEADS, GLA_DV, GLA_DKP), F32)
        g0_lat = jnp.pad(state_gla[:, l].swapaxes(-1, -2), kpad)
        os_, gf = [], []
        for d in range(2):
            o_c, fin_c = _gla_scan(p1, p2, w2p, b2p, g0_ctx, d, BATCH, SEQ, 0)
            o_l, _ = _gla_scan(p1, p2, w2p, b2p, g0_lat, d, DEC_BATCH, DEC_SEQ, T_CTX)
            os_.append(jnp.concatenate([o_c, o_l], axis=1))
            gf.append(fin_c[..., :GLA_DK].swapaxes(-1, -2))
        new_gla.append(jnp.stack(gf, axis=1))
        out_b = _head_post(os_[0], os_[1], p2, GLA_HEADS, gla_norm_g[l], 1, GLA_DV)

        lbp = hg_lb[l].reshape(2, HG_HEADS // 2, 1, MXU_W)
        e0_ctx = jnp.zeros((BATCH, 2, HG_HEADS, HG_DV, HG_EXPAND), F32)
        e0_lat = state_hgrn[:, l].swapaxes(-1, -2)
        os_, hf = [], []
        for d in range(2):
            o_c, fin_c = _hg_scan(p1, lbp, e0_ctx, d, BATCH, SEQ, 0, False)
            o_l, _ = _hg_scan(p1, lbp, e0_lat, d, DEC_BATCH, DEC_SEQ, T_CTX, True)
            os_.append(jnp.concatenate([o_c, o_l], axis=1))
            hf.append(fin_c.swapaxes(-1, -2))
        new_hg.append(jnp.stack(hf, axis=1))
        out_c = _head_post(os_[0], os_[1], p1, P1_HG, hg_norm_g[l], 2, HG_DV)

        wo = w_out[l].astype(BF16)
        x = _wout(out_a, out_b, out_c, wo[:S5_WIDTH], wo[S5_WIDTH:S5_WIDTH + GLA_WIDTH],
                  wo[S5_WIDTH + GLA_WIDTH:], x, mod5, l, 5)

        x = _ffn(x, mod5, norm_g[l, 2], l, 6,
                 ffn2_wg[l].astype(BF16), ffn2_wu[l].astype(BF16), ffn2_wd[l].astype(BF16))

    y = _final_norm(x, final_norm_g)
    y_prompt = y[:T_CTX].reshape(BATCH, SEQ, D_MODEL)
    y_sample = y[T_CTX:].reshape(DEC_BATCH, DEC_SEQ, D_MODEL)
    return (y_prompt, y_sample, jnp.stack(new_re, axis=1), jnp.stack(new_im, axis=1),
            jnp.stack(new_gla, axis=1), jnp.stack(new_hg, axis=1))
```

```python
import functools
import math

import jax
import jax.numpy as jnp
from jax import lax
from jax.experimental import pallas as pl
from jax.experimental.pallas import tpu as pltpu

F32 = jnp.float32
BF16 = jnp.bfloat16

D_MODEL = 4096
BATCH = 32
SEQ = 256
DEPTH = 2
DEC_BATCH = 2
DEC_SEQ = 4096
GRID_W = 64
S5_WIDTH = 1024
S5_CH = 16
S5_GROUPS = 64
S5_P = 64
GLA_WIDTH = 1536
GLA_HEADS = 4
GLA_DK = 192
GLA_DV = 384
GLA_RANK = 16
GLA_TAU = 16.0
HG_WIDTH = 1536
HG_EXPAND = 128
HG_HEADS = 12
HG_DV = 128
D_FF = 11008
N_MOD = 9
EPS = 1e-6
F_FLOOR = 1e-30

T_CTX = BATCH * SEQ
T_LAT = DEC_BATCH * DEC_SEQ
T_ALL = T_CTX + T_LAT
N_COND = 8
LANE = 128
S5_SUB = 8
MXU_W = 256
GLA_DKP = 256
S5_CB = 4
S5_GB = S5_GROUPS // S5_CB
S5_NB = S5_GB * S5_P
S5_NCB = 2
CHUNK = 256
LAT_CHUNKS = DEC_SEQ // CHUNK
N_ITEMS = BATCH + DEC_BATCH * LAT_CHUNKS
VMEM_LIMIT = 56 * 1024 * 1024
assert SEQ == CHUNK and T_CTX % DEC_SEQ == 0

TM_FFN = 1024
TF_FFN = 256
TM_MM = 1024
TM_WOUT = 512
TM_EW = 256
TM_HEAD = 1024
ROWS_EW = 64
COLS_DOWN = 512

PA_S5U = 0
PA_GQ = 4
PA_GK = 8
PA_LR = 12
PA_SLOTS = 16
PH_Q = 0
PH_Z = 6
PH_I = 18
PH_G = 24
PH_SLOTS = 30


def _cparams(sem):
    return pltpu.CompilerParams(dimension_semantics=sem, vmem_limit_bytes=VMEM_LIMIT)


def _mod_row(i, tm):
    start = i * tm
    return jnp.where(start < T_CTX, 0, 1 + (start - T_CTX) // DEC_SEQ)


def _mod_spec(layer, j, tm):
    return pl.BlockSpec((None, None, None, 1, D_MODEL),
                        lambda i, *_: (layer, _mod_row(i, tm), j, 0, 0))


def _item_rowblk(w, rev):
    j = jnp.maximum(w - BATCH, 0)
    c = j % LAT_CHUNKS
    if rev:
        c = LAT_CHUNKS - 1 - c
    return jnp.where(w < BATCH, w, BATCH + (j // LAT_CHUNKS) * LAT_CHUNKS + c)


def _item_cache(w):
    return jnp.maximum(w - BATCH, 0) // LAT_CHUNKS


def _item_fin(w):
    return jnp.minimum(w, BATCH)


def _item_init(w, st_ref, h0_ref):
    @pl.when(w < BATCH)
    def _():
        st_ref[...] = jnp.zeros(st_ref.shape, st_ref.dtype)

    @pl.when(jnp.logical_and(w >= BATCH, (w - BATCH) % LAT_CHUNKS == 0))
    def _():
        st_ref[...] = h0_ref[...]


def _sigmoid(x):
    return 1.0 / (1.0 + jnp.exp(-x))


def _silu(x):
    return x * _sigmoid(x)


def _norm_mod(x, g, sc, sh):
    ms = jnp.mean(x * x, axis=-1, keepdims=True)
    y = x * lax.rsqrt(ms + EPS) * g
    return y * (1.0 + sc) + sh


def _ada_kernel(c_ref, w_ref, b_ref, o_ref):
    s = _silu(c_ref[...]).astype(BF16)
    o_ref[...] = jnp.dot(s, w_ref[...].astype(BF16), preferred_element_type=F32) + b_ref[...]


def _ada(cond8, ada_w, ada_b):
    tn = 512
    n = N_MOD * D_MODEL
    return pl.pallas_call(
        _ada_kernel,
        grid=(DEPTH, n // tn),
        in_specs=[
            pl.BlockSpec((N_COND, D_MODEL), lambda l, j: (0, 0)),
            pl.BlockSpec((None, D_MODEL, tn), lambda l, j: (l, 0, j)),
            pl.BlockSpec((None, 1, tn), lambda l, j: (l, 0, j)),
        ],
        out_specs=pl.BlockSpec((None, N_COND, tn), lambda l, j: (l, 0, j)),
        out_shape=jax.ShapeDtypeStruct((DEPTH, N_COND, n), F32),
        compiler_params=_cparams(("parallel", "parallel")),
        name="ada",
    )(cond8, ada_w, ada_b.reshape(DEPTH, 1, n))


def _lb_kernel(raw_ref, o_ref):
    rows = [raw_ref[l] for l in range(DEPTH)]
    mx = functools.reduce(jnp.maximum, rows)
    es = [jnp.exp(r - mx) for r in rows]
    den = functools.reduce(lambda a, b: a + b, es)
    ps = [e / den for e in es]
    acc = ps[0]
    o_ref[0] = acc - ps[0]
    for l in range(1, DEPTH):
        acc = acc + ps[l]
        o_ref[l] = acc - ps[0]


def _hg_lb(raw):
    return pl.pallas_call(
        _lb_kernel,
        out_shape=jax.ShapeDtypeStruct(raw.shape, F32),
        name="hg_lb",
    )(raw)


def _ffn_kernel(x_ref, g_ref, sh_ref, sc_ref, gt_ref, wg_ref, wu_ref, wd_ref, o_ref, h_ref):
    @pl.when(pl.program_id(1) == 0)
    def _():
        def rows(r, carry):
            sl = pl.ds(pl.multiple_of(r * ROWS_EW, ROWS_EW), ROWS_EW)
            x = x_ref[sl, :]
            h_ref[sl, :] = _norm_mod(x, g_ref[...], sc_ref[...], sh_ref[...]).astype(BF16)
            o_ref[sl, :] = x
            return carry

        lax.fori_loop(0, x_ref.shape[0] // ROWS_EW, rows, 0)

    h = h_ref[...]
    a = jnp.dot(h, wg_ref[...], preferred_element_type=F32)
    u = jnp.dot(h, wu_ref[...], preferred_element_type=F32)
    act = (_silu(a) * u).astype(BF16)
    gate = 0.5 * gt_ref[...]
    for n in range(D_MODEL // COLS_DOWN):
        sl = slice(n * COLS_DOWN, (n + 1) * COLS_DOWN)
        o_ref[:, sl] += gate[:, sl] * jnp.dot(act, wd_ref[:, sl], preferred_element_type=F32)


def _ffn(x, mod5, gain, layer, jbase, wg, wu, wd):
    tm, tf = TM_FFN, TF_FFN
    resident = pl.Buffered(1)
    return pl.pallas_call(
        _ffn_kernel,
        grid=(T_ALL // tm, D_FF // tf),
        in_specs=[
            pl.BlockSpec((tm, D_MODEL), lambda i, f: (i, 0), pipeline_mode=resident),
            pl.BlockSpec((1, D_MODEL), lambda i, f: (0, 0)),
            _mod_spec(layer, jbase, tm),
            _mod_spec(layer, jbase + 1, tm),
            _mod_spec(layer, jbase + 2, tm),
            pl.BlockSpec((D_MODEL, tf), lambda i, f: (0, f)),
            pl.BlockSpec((D_MODEL, tf), lambda i, f: (0, f)),
            pl.BlockSpec((tf, D_MODEL), lambda i, f: (f, 0)),
        ],
        out_specs=pl.BlockSpec((tm, D_MODEL), lambda i, f: (i, 0), pipeline_mode=resident),
        out_shape=jax.ShapeDtypeStruct((T_ALL, D_MODEL), F32),
        scratch_shapes=[pltpu.VMEM((tm, D_MODEL), BF16)],
        compiler_params=_cparams(("parallel", "arbitrary")),
        name="ffn",
    )(x, gain.reshape(1, D_MODEL), mod5, mod5, mod5, wg, wu, wd)


def _normmod_kernel(x_ref, g_ref, sh_ref, sc_ref, o_ref):
    o_ref[...] = _norm_mod(x_ref[...], g_ref[...], sc_ref[...], sh_ref[...]).astype(BF16)


def _normmod(x, mod5, gain, layer, jbase):
    tm = TM_EW
    return pl.pallas_call(
        _normmod_kernel,
        grid=(T_ALL // tm,),
        in_specs=[
            pl.BlockSpec((tm, D_MODEL), lambda i: (i, 0)),
            pl.BlockSpec((1, D_MODEL), lambda i: (0, 0)),
            _mod_spec(layer, jbase, tm),
            _mod_spec(layer, jbase + 1, tm),
        ],
        out_specs=pl.BlockSpec((tm, D_MODEL), lambda i: (i, 0)),
        out_shape=jax.ShapeDtypeStruct((T_ALL, D_MODEL), BF16),
        compiler_params=_cparams(("parallel",)),
        name="normmod",
    )(x, gain.reshape(1, D_MODEL), mod5, mod5)


def _rmsnorm_kernel(x_ref, g_ref, o_ref):
    x = x_ref[...]
    ms = jnp.mean(x * x, axis=-1, keepdims=True)
    o_ref[...] = x * lax.rsqrt(ms + EPS) * g_ref[...]


def _final_norm(x, gain, row0, rows):
    tm = TM_EW
    blk0 = row0 // tm
    return pl.pallas_call(
        _rmsnorm_kernel,
        grid=(rows // tm,),
        in_specs=[pl.BlockSpec((tm, D_MODEL), lambda i: (blk0 + i, 0)),
                  pl.BlockSpec((1, D_MODEL), lambda i: (0, 0))],
        out_specs=pl.BlockSpec((tm, D_MODEL), lambda i: (i, 0)),
        out_shape=jax.ShapeDtypeStruct((rows, D_MODEL), F32),
        compiler_params=_cparams(("parallel",)),
        name="final_norm",
    )(x, gain.reshape(1, D_MODEL))


def _proj_kernel(a_ref, w_ref, o_ref, *, g, tw):
    acc = jnp.dot(a_ref[...], w_ref[...], preferred_element_type=F32)
    for s in range(g):
        o_ref[s] = acc[:, s * tw:(s + 1) * tw]


def _proj(h, w, tw, g):
    n = w.shape[1]
    tm = TM_MM
    return pl.pallas_call(
        functools.partial(_proj_kernel, g=g, tw=tw),
        grid=(T_ALL // tm, n // (g * tw)),
        in_specs=[pl.BlockSpec((tm, D_MODEL), lambda i, j: (i, 0)),
                  pl.BlockSpec((D_MODEL, g * tw), lambda i, j: (0, j))],
        out_specs=pl.BlockSpec((g, tm, tw), lambda i, j: (j, i, 0)),
        out_shape=jax.ShapeDtypeStruct((n // tw, T_ALL, tw), F32),
        compiler_params=_cparams(("parallel", "arbitrary")),
        name="proj",
    )(h, w)


def _s5disc_kernel(are_ref, aim_ref, ldt_ref, bre_ref, bim_ref, lre_ref, lim_ref, bbre_ref, bbim_ref):
    a_re = are_ref[...]
    a_im = aim_ref[...]
    dt = jnp.exp(ldt_ref[...])
    mag = jnp.exp(a_re * dt)
    lam_re = mag * jnp.cos(a_im * dt)
    lam_im = mag * jnp.sin(a_im * dt)
    den = a_re * a_re + a_im * a_im
    z_re = ((lam_re - 1.0) * a_re + lam_im * a_im) / den
    z_im = (lam_im * a_re - (lam_re - 1.0) * a_im) / den
    b_re = bre_ref[...]
    b_im = bim_ref[...]
    lre_ref[...] = lam_re
    lim_ref[...] = lam_im
    bbre_ref[...] = z_re * b_re - z_im * b_im
    bbim_ref[...] = z_re * b_im + z_im * b_re


def _s5_discretise(a_re, a_im, log_dt, b_re, b_im):
    shp = (2, S5_GROUPS, S5_CH, S5_P)
    n = 2 * S5_GROUPS * S5_CH
    rep = lambda t: jnp.broadcast_to(t[:, :, None, :], shp).reshape(n, S5_P)
    ldt = jnp.broadcast_to(log_dt[:, :, None, None], shp).reshape(n, S5_P)
    tr = lambda t: t.swapaxes(-1, -2).reshape(n, S5_P)
    lam_re, lam_im, bb_re, bb_im = pl.pallas_call(
        _s5disc_kernel,
        out_shape=[jax.ShapeDtypeStruct((n, S5_P), F32)] * 4,
        name="s5_disc",
    )(rep(a_re), rep(a_im), ldt, tr(b_re), tr(b_im))
    return (lam_re.reshape(shp)[:, :, 0], lam_im.reshape(shp)[:, :, 0],
            bb_re.reshape(shp), bb_im.reshape(shp))


def _s5_kernel(uf_ref, ub_ref, bbd_ref, cbd_ref, lam_ref, h0_ref, yf_ref, yb_ref, fin_ref,
               st_ref, bu_ref, x_ref):
    _item_init(pl.program_id(1), st_ref, h0_ref)
    chains = [(d, j) for d in range(2) for j in range(S5_NCB)]
    u_refs = (uf_ref, ub_ref)
    y_refs = (yf_ref, yb_ref)

    sub = lambda s: pl.ds(s, CHUNK, stride=S5_SUB)
    for d, j in chains:
        bu = jnp.dot(u_refs[d][j].astype(BF16), bbd_ref[d, j], preferred_element_type=F32)
        for part in range(2):
            for s in range(S5_SUB):
                col = part * S5_NB + s * LANE
                bu_ref[d, j, part, sub(s), :] = bu[:, col:col + LANE]
    lam = [(lam_ref[d, j, 0], lam_ref[d, j, 1]) for d, j in chains]

    def step(t, carry):
        new = []
        for (d, j), (lam_re, lam_im), (xr, xi) in zip(chains, lam, carry):
            tt = CHUNK - 1 - t if d == 1 else t
            rows = pl.ds(pl.multiple_of(tt * S5_SUB, S5_SUB), S5_SUB)
            nr = lam_re * xr - lam_im * xi + bu_ref[d, j, 0, rows, :]
            ni = lam_re * xi + lam_im * xr + bu_ref[d, j, 1, rows, :]
            x_ref[d, j, 0, rows, :] = nr
            x_ref[d, j, 1, rows, :] = ni
            new.append((nr, ni))
        return tuple(new)

    init = tuple((st_ref[d, j, 0], st_ref[d, j, 1]) for d, j in chains)
    fin = lax.fori_loop(0, CHUNK, step, init, unroll=8)
    for (d, j), (xr, xi) in zip(chains, fin):
        st_ref[d, j, 0] = xr
        st_ref[d, j, 1] = xi
        fin_ref[d, j, 0] = xr
        fin_ref[d, j, 1] = xi
    for d, j in chains:
        x = jnp.concatenate([x_ref[d, j, part, sub(s), :] for part in range(2) for s in range(S5_SUB)], axis=1)
        y_refs[d][j] = jnp.dot(x.astype(BF16), cbd_ref[d, j], preferred_element_type=F32)


def _s5_scan(pa, bbd, cbd, lam, cache):
    c, nb = CHUNK, S5_NCB
    tile = (2, S5_SUB, LANE)
    uspec = lambda rev: pl.BlockSpec((nb, c, MXU_W), lambda g, w: (PA_S5U // nb + g, _item_rowblk(w, rev), 0))
    yspec = lambda rev: pl.BlockSpec((nb, c, MXU_W), lambda g, w: (g, _item_rowblk(w, rev), 0))
    return pl.pallas_call(
        _s5_kernel,
        grid=(S5_CB // nb, N_ITEMS),
        in_specs=[
            uspec(False),
            uspec(True),
            pl.BlockSpec((2, nb, MXU_W, 2 * S5_NB), lambda g, w: (0, g, 0, 0)),
            pl.BlockSpec((2, nb, 2 * S5_NB, MXU_W), lambda g, w: (0, g, 0, 0)),
            pl.BlockSpec((2, nb) + tile, lambda g, w: (0, g, 0, 0, 0)),
            pl.BlockSpec((None, 2, nb) + tile, lambda g, w: (_item_cache(w), 0, g, 0, 0, 0)),
        ],
        out_specs=[
            yspec(False),
            yspec(True),
            pl.BlockSpec((None, 2, nb) + tile, lambda g, w: (_item_fin(w), 0, g, 0, 0, 0)),
        ],
        out_shape=[jax.ShapeDtypeStruct((S5_CB, T_ALL, MXU_W), F32),
                   jax.ShapeDtypeStruct((S5_CB, T_ALL, MXU_W), F32),
                   jax.ShapeDtypeStruct((BATCH + 1, 2, S5_CB) + tile, F32)],
        scratch_shapes=[pltpu.VMEM((2, nb) + tile, F32),
                        pltpu.VMEM((2, nb, 2, c * S5_SUB, LANE), F32),
                        pltpu.VMEM((2, nb, 2, c * S5_SUB, LANE), F32)],
        compiler_params=_cparams(("parallel", "arbitrary")),
        name="s5_scan",
    )(pa, pa, bbd, cbd, lam, cache)


def _gelu_tanh(x):
    return 0.5 * x * (1.0 + jnp.tanh(math.sqrt(2.0 / math.pi) * (x + 0.044715 * x * x * x)))


def _s5post_kernel(yf_ref, yb_ref, u_ref, d_ref, w_ref, b_ref, o_ref):
    ya = []
    for cb in range(S5_CB):
        ya.append(_gelu_tanh(yf_ref[cb] + yb_ref[cb] + d_ref[cb] * u_ref[cb]))
    z = b_ref[...]
    for cb in range(S5_CB):
        z = z + jnp.dot(ya[cb].astype(BF16), w_ref[cb * MXU_W:(cb + 1) * MXU_W, :],
                        preferred_element_type=F32)
    for cb in range(S5_CB):
        sl = slice(cb * MXU_W, (cb + 1) * MXU_W)
        o_ref[:, sl] = (ya[cb] * _sigmoid(z[:, sl])).astype(BF16)


def _s5_post(yf, yb, pa, skip, glu_w, glu_b):
    tm = TM_EW
    return pl.pallas_call(
        _s5post_kernel,
        grid=(T_ALL // tm,),
        in_specs=[
            pl.BlockSpec((S5_CB, tm, MXU_W), lambda i: (0, i, 0)),
            pl.BlockSpec((S5_CB, tm, MXU_W), lambda i: (0, i, 0)),
            pl.BlockSpec((S5_CB, tm, MXU_W), lambda i: (0, i, 0)),
            pl.BlockSpec((S5_CB, 1, MXU_W), lambda i: (0, 0, 0)),
            pl.BlockSpec((S5_WIDTH, S5_WIDTH), lambda i: (0, 0)),
            pl.BlockSpec((1, S5_WIDTH), lambda i: (0, 0)),
        ],
        out_specs=pl.BlockSpec((tm, S5_WIDTH), lambda i: (i, 0)),
        out_shape=jax.ShapeDtypeStruct((T_ALL, S5_WIDTH), BF16),
        compiler_params=_cparams(("parallel",)),
        name="s5_post",
    )(yf, yb, pa, skip, glu_w, glu_b)


def _level_matrix(c, rev):
    ri = lax.broadcasted_iota(jnp.int32, (c, c), 0)
    ci = lax.broadcasted_iota(jnp.int32, (c, c), 1)
    x = ri ^ ci
    lvl = jnp.full((c, c), -1, jnp.int32)
    for l in range(int(math.log2(c))):
        lvl = jnp.where((x >> l) != 0, l, lvl)
    before = (ci >= ri) if rev else (ci <= ri)
    return jnp.where(before, lvl, 99)


def _gla_chunk(q, k, v, lg, st, lvlmat, rev):
    c, kk = q.shape
    rowk = lax.broadcasted_iota(jnp.int32, (c, kk), 0)
    nt = (((1,), (1,)), ((), ()))
    p = lg
    r = jnp.zeros_like(lg)
    a = jnp.where(lvlmat == -1, jnp.sum(q * k, axis=-1, keepdims=True), 0.0)
    for lvl in range(int(math.log2(c))):
        m = 1 << lvl
        qs = (q * jnp.exp(p)).astype(BF16)
        ks = (k * jnp.exp(r)).astype(BF16)
        al = lax.dot_general(qs, ks, nt, preferred_element_type=F32)
        a = jnp.where(lvlmat == lvl, al, a)
        hi = (rowk & m) != 0
        tot = p + r
        t_prev = pltpu.roll(tot, m, 0)
        t_next = pltpu.roll(tot, c - m, 0)
        if rev:
            p = p + jnp.where(hi, 0.0, t_next)
            r = r + jnp.where(hi, t_prev, 0.0)
        else:
            p = p + jnp.where(hi, t_prev, 0.0)
            r = r + jnp.where(hi, 0.0, t_next)
    qe = (q * jnp.exp(p)).astype(BF16)
    ke = (k * jnp.exp(r)).astype(BF16)
    vb = v.astype(BF16)
    o = jnp.dot(a.astype(BF16), vb, preferred_element_type=F32)
    o = o + lax.dot_general(qe, st.astype(BF16), nt, preferred_element_type=F32)
    decay = jnp.exp((p + r)[0:1, :])
    st_new = st * decay + lax.dot_general(vb, ke, (((0,), (0,)), ((), ())), preferred_element_type=F32)
    return o, st_new


def _log_sigmoid(z):
    return jnp.minimum(z, 0.0) - jnp.log(1.0 + jnp.exp(-jnp.abs(z)))


def _level_matrix_once(lvl_ref, rev):
    @pl.when(pl.program_id(0) == 0)
    def _():
        lvl_ref[...] = _level_matrix(CHUNK, rev)

    return lvl_ref[...]


def _gla_kernel(q_ref, k_ref, v_ref, lr_ref, w2_ref, b2_ref, h0_ref, o_ref, fin_ref, st_ref, lvl_ref, *, rev):
    _item_init(pl.program_id(0), st_ref, h0_ref)
    lr = lr_ref[...].astype(BF16)
    lvlmat = _level_matrix_once(lvl_ref, rev)

    def head(h, carry):
        z = jnp.dot(lr, w2_ref[h], preferred_element_type=F32) + b2_ref[h]
        lg = _log_sigmoid(z) * (1.0 / GLA_TAU)
        q = q_ref[h] * (GLA_DK ** -0.5)
        o, st = _gla_chunk(q, k_ref[h], v_ref[h], lg, st_ref[h], lvlmat, rev)
        o_ref[h] = o
        st_ref[h] = st
        fin_ref[h] = st
        return carry

    lax.fori_loop(0, GLA_HEADS, head, 0)


def _gla_scan(pa, p2, w2p, b2p, cache, d):
    c = CHUNK
    rev = d == 1
    hq = GLA_HEADS
    return pl.pallas_call(
        functools.partial(_gla_kernel, rev=rev),
        grid=(N_ITEMS,),
        in_specs=[
            pl.BlockSpec((hq, c, GLA_DKP), lambda w: (PA_GQ // hq, _item_rowblk(w, rev), 0)),
            pl.BlockSpec((hq, c, GLA_DKP), lambda w: (PA_GK // hq, _item_rowblk(w, rev), 0)),
            pl.BlockSpec((hq, c, GLA_DV), lambda w: (0, _item_rowblk(w, rev), 0)),
            pl.BlockSpec((None, c, MXU_W), lambda w: (PA_LR, _item_rowblk(w, rev), 0)),
            pl.BlockSpec((None, hq, MXU_W, GLA_DKP), lambda w: (d, 0, 0, 0)),
            pl.BlockSpec((None, hq, 1, GLA_DKP), lambda w: (d, 0, 0, 0)),
            pl.BlockSpec((None, None, hq, GLA_DV, GLA_DKP), lambda w: (_item_cache(w), d, 0, 0, 0)),
        ],
        out_specs=[
            pl.BlockSpec((hq, c, GLA_DV), lambda w: (0, _item_rowblk(w, rev), 0)),
            pl.BlockSpec((None, hq, GLA_DV, GLA_DKP), lambda w: (_item_fin(w), 0, 0, 0)),
        ],
        out_shape=[jax.ShapeDtypeStruct((hq, T_ALL, GLA_DV), F32),
                   jax.ShapeDtypeStruct((BATCH + 1, hq, GLA_DV, GLA_DKP), F32)],
        scratch_shapes=[pltpu.VMEM((hq, GLA_DV, GLA_DKP), F32), pltpu.VMEM((c, c), jnp.int32)],
        compiler_params=_cparams(("arbitrary",)),
        name="gla_scan",
    )(pa, pa, p2, pa, w2p, b2p, cache)


def _hg_kernel(q_ref, z_ref, v_ref, lb_ref, h0_ref, o_ref, fin_ref, st_ref, lvl_ref, *, rev):
    _item_init(pl.program_id(0), st_ref, h0_ref)
    kk = HG_EXPAND
    lvlmat = _level_matrix_once(lvl_ref, rev)

    def pair(pr, carry):
        q2 = q_ref[pr] * (HG_EXPAND ** -0.5)
        v2 = v_ref[pr]
        lb = lb_ref[pr]
        f = lb + (1.0 - lb) * _sigmoid(z_ref[pr])
        lg2 = jnp.log(jnp.maximum(f, F_FLOOR))
        k2 = 1.0 - f
        outs = []
        for s in range(2):
            sl = slice(s * kk, (s + 1) * kk)
            h = 2 * pr + s
            o, st = _gla_chunk(q2[:, sl], k2[:, sl], v2[:, sl], lg2[:, sl], st_ref[h], lvlmat, rev)
            st_ref[h] = st
            fin_ref[h] = st
            outs.append(o)
        o_ref[pr] = jnp.concatenate(outs, axis=1)
        return carry

    lax.fori_loop(0, HG_HEADS // 2, pair, 0)


def _hg_scan(ph, lbp, cache, d):
    c = CHUNK
    rev = d == 1
    np_ = HG_HEADS // 2
    ispec = lambda slot: pl.BlockSpec((np_, c, MXU_W), lambda w: (slot // np_, _item_rowblk(w, rev), 0))
    return pl.pallas_call(
        functools.partial(_hg_kernel, rev=rev),
        grid=(N_ITEMS,),
        in_specs=[
            ispec(PH_Q),
            ispec(PH_Z + d * np_),
            ispec(PH_I),
            pl.BlockSpec((None, np_, 1, MXU_W), lambda w: (d, 0, 0, 0)),
            pl.BlockSpec((None, None, HG_HEADS, HG_DV, HG_EXPAND), lambda w: (_item_cache(w), d, 0, 0, 0)),
        ],
        out_specs=[
            pl.BlockSpec((np_, c, MXU_W), lambda w: (0, _item_rowblk(w, rev), 0)),
            pl.BlockSpec((None, HG_HEADS, HG_DV, HG_EXPAND), lambda w: (_item_fin(w), 0, 0, 0)),
        ],
        out_shape=[jax.ShapeDtypeStruct((np_, T_ALL, MXU_W), F32),
                   jax.ShapeDtypeStruct((BATCH + 1, HG_HEADS, HG_DV, HG_EXPAND), F32)],
        scratch_shapes=[pltpu.VMEM((HG_HEADS, HG_DV, HG_EXPAND), F32), pltpu.VMEM((c, c), jnp.int32)],
        compiler_params=_cparams(("arbitrary",)),
        name="hg_scan",
    )(ph, ph, ph, lbp, cache)


def _headpost_kernel(of_ref, ob_ref, gate_ref, gn_ref, out_ref, *, hp, v):
    gn = gn_ref[...]
    for s in range(hp):
        sl = slice(s * v, (s + 1) * v)
        o = of_ref[:, sl] + ob_ref[:, sl]
        y = o * lax.rsqrt(jnp.mean(o * o, axis=-1, keepdims=True) + EPS) * gn
        out_ref[:, sl] = (y * _silu(gate_ref[:, sl])).astype(BF16)


def _head_post(o_f, o_b, gate_src, gate_slot0, gain, hp, v):
    tm = TM_HEAD
    nh, _, w = o_f.shape
    return pl.pallas_call(
        functools.partial(_headpost_kernel, hp=hp, v=v),
        grid=(T_ALL // tm, nh),
        in_specs=[
            pl.BlockSpec((None, tm, w), lambda i, j: (j, i, 0)),
            pl.BlockSpec((None, tm, w), lambda i, j: (j, i, 0)),
            pl.BlockSpec((None, tm, w), lambda i, j: (gate_slot0 + j, i, 0)),
            pl.BlockSpec((1, v), lambda i, j: (0, 0)),
        ],
        out_specs=pl.BlockSpec((tm, w), lambda i, j: (i, j)),
        out_shape=jax.ShapeDtypeStruct((T_ALL, nh * w), BF16),
        compiler_params=_cparams(("parallel", "parallel")),
        name="head_post",
    )(o_f, o_b, gate_src, gain.reshape(1, v))


def _wout_kernel(a_ref, b_ref, c_ref, wa_ref, wb_ref, wc_ref, x_ref, gt_ref, o_ref):
    acc = jnp.dot(a_ref[...], wa_ref[...], preferred_element_type=F32)
    acc = acc + jnp.dot(b_ref[...], wb_ref[...], preferred_element_type=F32)
    acc = acc + jnp.dot(c_ref[...], wc_ref[...], preferred_element_type=F32)
    o_ref[...] = x_ref[...] + gt_ref[...] * acc


def _wout(ma, mb, mc, wa, wb, wc, x, mod5, layer, jgate):
    tm, tn = TM_WOUT, 1024
    return pl.pallas_call(
        _wout_kernel,
        grid=(T_ALL // tm, D_MODEL // tn),
        in_specs=[
            pl.BlockSpec((tm, S5_WIDTH), lambda i, j: (i, 0)),
            pl.BlockSpec((tm, GLA_WIDTH), lambda i, j: (i, 0)),
            pl.BlockSpec((tm, HG_WIDTH), lambda i, j: (i, 0)),
            pl.BlockSpec((S5_WIDTH, tn), lambda i, j: (0, j)),
            pl.BlockSpec((GLA_WIDTH, tn), lambda i, j: (0, j)),
            pl.BlockSpec((HG_WIDTH, tn), lambda i, j: (0, j)),
            pl.BlockSpec((tm, tn), lambda i, j: (i, j)),
            pl.BlockSpec((None, None, None, 1, tn), lambda i, j: (layer, _mod_row(i, tm), jgate, 0, j)),
        ],
        out_specs=pl.BlockSpec((tm, tn), lambda i, j: (i, j)),
        out_shape=jax.ShapeDtypeStruct((T_ALL, D_MODEL), F32),
        compiler_params=_cparams(("parallel", "arbitrary")),
        name="wout",
    )(ma, mb, mc, wa, wb, wc, x, mod5)


def _pad_heads(w, nh, dk, dkp):
    return jnp.pad(w.reshape(w.shape[0], nh, dk), ((0, 0), (0, 0), (0, dkp - dk))).reshape(w.shape[0], nh * dkp)


def _split_w_in(w):
    offs = [0]
    for s in (S5_WIDTH, GLA_HEADS * GLA_DK, GLA_HEADS * GLA_DK, GLA_WIDTH, GLA_WIDTH, 2 * GLA_RANK,
              HG_WIDTH, HG_WIDTH, HG_WIDTH, HG_WIDTH, HG_WIDTH):
        offs.append(offs[-1] + s)
    (u_a, q_b, k_b, v_b, g_b, lr_b, q_c, zf_c, zb_c, i_c, g_c) = [w[:, offs[i]:offs[i + 1]] for i in range(11)]
    wa = jnp.concatenate([
        u_a,
        _pad_heads(q_b, GLA_HEADS, GLA_DK, GLA_DKP),
        _pad_heads(k_b, GLA_HEADS, GLA_DK, GLA_DKP),
        jnp.pad(lr_b, ((0, 0), (0, (PA_SLOTS - PA_LR) * MXU_W - 2 * GLA_RANK))),
    ], axis=1).astype(BF16)
    wh = jnp.concatenate([q_c, zf_c, zb_c, i_c, g_c], axis=1).astype(BF16)
    w2 = jnp.concatenate([v_b, g_b], axis=1).astype(BF16)
    return wa, wh, w2


def _lat_grid_transpose(t):
    w = t.shape[1]
    rows = DEC_SEQ // GRID_W
    lat = t[T_CTX:].reshape(DEC_BATCH, rows, GRID_W, w).swapaxes(1, 2).reshape(T_LAT, w)
    return jnp.concatenate([t[:T_CTX], lat], axis=0)


def _s5_blockdiag(bb_re, bb_im, c_re, c_im):
    eye = jnp.eye(S5_GB, dtype=F32)

    def bmat(bb):
        t = bb.reshape(2, S5_CB, S5_GB, S5_CH, S5_P)
        return jnp.einsum('dbgcp,gh->dbgchp', t, eye).reshape(2, S5_CB, S5_GB * S5_CH, S5_NB)

    def cmat(cc):
        t = cc.reshape(2, S5_CB, S5_GB, S5_CH, S5_P)
        return jnp.einsum('dbgcp,gh->dbgphc', t, eye).reshape(2, S5_CB, S5_NB, S5_GB * S5_CH)

    bbd = jnp.concatenate([bmat(bb_re), bmat(bb_im)], axis=3).astype(BF16)
    cbd = jnp.concatenate([cmat(c_re), -cmat(c_im)], axis=2).astype(BF16)
    return bbd, cbd


def _s5_state_in(re, im):
    lead = re.shape[:-2]
    f = lambda t: t.reshape(lead + (S5_CB, 1, S5_SUB, LANE))
    return jnp.concatenate([f(re), f(im)], axis=-3)


def kernel(x_prompt, x_sample, state_s5_re, state_s5_im, state_gla, state_hgrn, c, c_ctx, ada_w, ada_b, norm_g, ffn1_wg, ffn1_wu, ffn1_wd, ffn2_wg, ffn2_wu, ffn2_wd, w_in, w_out, s5_a_re, s5_a_im, s5_log_dt, s5_b_re, s5_b_im, s5_c_re, s5_c_im, s5_d, s5_glu_w, s5_glu_b, gla_w2, gla_b2, gla_norm_g, hg_lb_raw, hg_norm_g, final_norm_g):
    x = jnp.concatenate([x_prompt.reshape(T_CTX, D_MODEL), x_sample.reshape(T_LAT, D_MODEL)], axis=0)
    cond8 = jnp.zeros((N_COND, D_MODEL), F32).at[0].set(c_ctx).at[1:1 + DEC_BATCH].set(c)
    mod5 = _ada(cond8, ada_w, ada_b).reshape(DEPTH, N_COND, N_MOD, 1, D_MODEL)
    hg_lb = _hg_lb(hg_lb_raw.astype(F32).swapaxes(0, 1))

    new_re, new_im, new_gla, new_hg = [], [], [], []
    for l in range(DEPTH):
        x = _ffn(x, mod5, norm_g[l, 0], l, 0,
                 ffn1_wg[l].astype(BF16), ffn1_wu[l].astype(BF16), ffn1_wd[l].astype(BF16))

        h = _normmod(x, mod5, norm_g[l, 1], l, 3)
        wa, wh, w2 = _split_w_in(w_in[l])
        pa = _proj(h, wa, MXU_W, 4)
        p2 = _proj(h, w2, GLA_DV, 2)
        ph = _proj(_lat_grid_transpose(h), wh, MXU_W, 3)

        lam_re, lam_im, bb_re, bb_im = _s5_discretise(s5_a_re[l], s5_a_im[l], s5_log_dt[l], s5_b_re[l], s5_b_im[l])
        bbd, cbd = _s5_blockdiag(bb_re, bb_im, s5_c_re[l].astype(F32), s5_c_im[l].astype(F32))
        lam = _s5_state_in(lam_re, lam_im)
        s5_cache = _s5_state_in(state_s5_re[:, l], state_s5_im[:, l])
        y_f, y_b, s5_fin = _s5_scan(pa, bbd, cbd, lam, s5_cache)
        new_re.append(s5_fin[:BATCH, :, :, 0].reshape(BATCH, 2, S5_GROUPS, S5_P))
        new_im.append(s5_fin[:BATCH, :, :, 1].reshape(BATCH, 2, S5_GROUPS, S5_P))
        out_a = _s5_post(y_f, y_b, pa, s5_d[l].reshape(S5_CB, 1, MXU_W),
                         s5_glu_w[l].astype(BF16), s5_glu_b[l].reshape(1, S5_WIDTH))

        w2p = jnp.zeros((2, MXU_W, GLA_HEADS, GLA_DKP), F32)
        for d in range(2):
            w2p = w2p.at[d, d * GLA_RANK:(d + 1) * GLA_RANK, :, :GLA_DK].set(
                gla_w2[l, d].reshape(GLA_RANK, GLA_HEADS, GLA_DK))
        w2p = w2p.transpose(0, 2, 1, 3).astype(BF16)
        b2p = jnp.pad(gla_b2[l].reshape(2, GLA_HEADS, 1, GLA_DK), ((0, 0), (0, 0), (0, 0), (0, GLA_DKP - GLA_DK)))
        kpad = ((0, 0),) * 4 + ((0, GLA_DKP - GLA_DK),)
        gla_cache = jnp.pad(state_gla[:, l].swapaxes(-1, -2), kpad)
        os_, gf = [], []
        for d in range(2):
            o_d, fin_d = _gla_scan(pa, p2, w2p, b2p, gla_cache, d)
            os_.append(o_d)
            gf.append(fin_d[:BATCH, ..., :GLA_DK].swapaxes(-1, -2))
        new_gla.append(jnp.stack(gf, axis=1))
        out_b = _head_post(os_[0], os_[1], p2, GLA_HEADS, gla_norm_g[l], 1, GLA_DV)

        lbp = hg_lb[l].reshape(2, HG_HEADS // 2, 1, MXU_W)
        hg_cache = state_hgrn[:, l].swapaxes(-1, -2)
        os_, hf = [], []
        for d in range(2):
            o_d, fin_d = _hg_scan(ph, lbp, hg_cache, d)
            os_.append(o_d)
            hf.append(fin_d[:BATCH].swapaxes(-1, -2))
        new_hg.append(jnp.stack(hf, axis=1))
        out_c = _lat_grid_transpose(_head_post(os_[0], os_[1], ph, PH_G, hg_norm_g[l], 2, HG_DV))

        wo = w_out[l]
        x = _wout(out_a, out_b, out_c, wo[:S5_WIDTH].astype(BF16),
                  wo[S5_WIDTH:S5_WIDTH + GLA_WIDTH].astype(BF16), wo[S5_WIDTH + GLA_WIDTH:].astype(BF16),
                  x, mod5, l, 5)

        x = _ffn(x, mod5, norm_g[l, 2], l, 6,
                 ffn2_wg[l].astype(BF16), ffn2_wu[l].astype(BF16), ffn2_wd[l].astype(BF16))

    y_prompt = _final_norm(x, final_norm_g, 0, T_CTX).reshape(BATCH, SEQ, D_MODEL)
    y_sample = _final_norm(x, final_norm_g, T_CTX, T_LAT).reshape(DEC_BATCH, DEC_SEQ, D_MODEL)
    return (y_prompt, y_sample, jnp.stack(new_re, axis=1), jnp.stack(new_im, axis=1),
            jnp.stack(new_gla, axis=1), jnp.stack(new_hg, axis=1))
```

```python
import functools
import math

import jax
import jax.numpy as jnp
from jax import lax
from jax.experimental import pallas as pl
from jax.experimental.pallas import tpu as pltpu

F32 = jnp.float32
BF16 = jnp.bfloat16

D_MODEL = 4096
BATCH = 32
SEQ = 256
DEPTH = 2
DEC_BATCH = 2
DEC_SEQ = 4096
GRID_W = 64
S5_WIDTH = 1024
S5_CH = 16
S5_GROUPS = 64
S5_P = 64
GLA_WIDTH = 1536
GLA_HEADS = 4
GLA_DK = 192
GLA_DV = 384
GLA_RANK = 16
GLA_TAU = 16.0
HG_WIDTH = 1536
HG_EXPAND = 128
HG_HEADS = 12
HG_DV = 128
D_FF = 11008
N_MOD = 9
EPS = 1e-6
F_FLOOR = 1e-30

T_CTX = BATCH * SEQ
T_LAT = DEC_BATCH * DEC_SEQ
T_ALL = T_CTX + T_LAT
N_COND = 8
LANE = 128
S5_SUB = 8
MXU_W = 256
GLA_DKP = 256
S5_CB = 4
S5_GB = S5_GROUPS // S5_CB
S5_NB = S5_GB * S5_P
S5_NCB = 2
CHUNK = 256
LAT_CHUNKS = DEC_SEQ // CHUNK
N_ITEMS = BATCH + DEC_BATCH * LAT_CHUNKS
VMEM_LIMIT = 56 * 1024 * 1024
assert SEQ == CHUNK and T_CTX % DEC_SEQ == 0

TM_FFN = 1024
TF_FFN = 256
TM_MM = 1024
TM_WOUT = 512
TM_EW = 256
TM_HEAD = 1024
ROWS_EW = 64
COLS_DOWN = 512

PA_S5U = 0
PA_GQ = 4
PA_GK = 8
PA_SLOTS = 12
PH_Q = 0
PH_Z = 6
PH_I = 18
PH_G = 24
PH_SLOTS = 30


def _cparams(sem):
    return pltpu.CompilerParams(dimension_semantics=sem, vmem_limit_bytes=VMEM_LIMIT)


def _mod_row(i, tm):
    start = i * tm
    return jnp.where(start < T_CTX, 0, 1 + (start - T_CTX) // DEC_SEQ)


def _mod_spec(layer, j, tm):
    return pl.BlockSpec((None, None, None, 1, D_MODEL),
                        lambda i, *_: (layer, _mod_row(i, tm), j, 0, 0))


def _item_rowblk(w, rev):
    j = jnp.maximum(w - BATCH, 0)
    c = j % LAT_CHUNKS
    if rev:
        c = LAT_CHUNKS - 1 - c
    return jnp.where(w < BATCH, w, BATCH + (j // LAT_CHUNKS) * LAT_CHUNKS + c)


def _item_cache(w):
    return jnp.maximum(w - BATCH, 0) // LAT_CHUNKS


def _item_fin(w):
    return jnp.minimum(w, BATCH)


def _item_init(w, st_ref, h0_ref):
    @pl.when(w < BATCH)
    def _():
        st_ref[...] = jnp.zeros(st_ref.shape, st_ref.dtype)

    @pl.when(jnp.logical_and(w >= BATCH, (w - BATCH) % LAT_CHUNKS == 0))
    def _():
        st_ref[...] = h0_ref[...]


def _sigmoid(x):
    return 1.0 / (1.0 + jnp.exp(-x))


def _silu(x):
    return x * _sigmoid(x)


def _norm_mod(x, g, sc, sh):
    ms = jnp.mean(x * x, axis=-1, keepdims=True)
    y = x * lax.rsqrt(ms + EPS) * g
    return y * (1.0 + sc) + sh


def _ada_kernel(c_ref, w_ref, b_ref, o_ref):
    s = _silu(c_ref[...]).astype(BF16)
    o_ref[...] = jnp.dot(s, w_ref[...].astype(BF16), preferred_element_type=F32) + b_ref[...]


def _ada(cond8, ada_w, ada_b):
    tn = 512
    n = N_MOD * D_MODEL
    return pl.pallas_call(
        _ada_kernel,
        grid=(DEPTH, n // tn),
        in_specs=[
            pl.BlockSpec((N_COND, D_MODEL), lambda l, j: (0, 0)),
            pl.BlockSpec((None, D_MODEL, tn), lambda l, j: (l, 0, j)),
            pl.BlockSpec((None, 1, tn), lambda l, j: (l, 0, j)),
        ],
        out_specs=pl.BlockSpec((None, N_COND, tn), lambda l, j: (l, 0, j)),
        out_shape=jax.ShapeDtypeStruct((DEPTH, N_COND, n), F32),
        compiler_params=_cparams(("parallel", "parallel")),
        name="ada",
    )(cond8, ada_w, ada_b.reshape(DEPTH, 1, n))


def _lb_kernel(raw_ref, o_ref):
    rows = [raw_ref[l] for l in range(DEPTH)]
    mx = functools.reduce(jnp.maximum, rows)
    es = [jnp.exp(r - mx) for r in rows]
    den = functools.reduce(lambda a, b: a + b, es)
    ps = [e / den for e in es]
    acc = ps[0]
    o_ref[0] = acc - ps[0]
    for l in range(1, DEPTH):
        acc = acc + ps[l]
        o_ref[l] = acc - ps[0]


def _hg_lb(raw):
    return pl.pallas_call(
        _lb_kernel,
        out_shape=jax.ShapeDtypeStruct(raw.shape, F32),
        name="hg_lb",
    )(raw)


def _ffn_kernel(x_ref, g_ref, sh_ref, sc_ref, gt_ref, wg_ref, wu_ref, wd_ref, o_ref, h_ref):
    @pl.when(pl.program_id(1) == 0)
    def _():
        def rows(r, carry):
            sl = pl.ds(pl.multiple_of(r * ROWS_EW, ROWS_EW), ROWS_EW)
            x = x_ref[sl, :]
            h_ref[sl, :] = _norm_mod(x, g_ref[...], sc_ref[...], sh_ref[...]).astype(BF16)
            o_ref[sl, :] = x
            return carry

        lax.fori_loop(0, x_ref.shape[0] // ROWS_EW, rows, 0)

    h = h_ref[...]
    a = jnp.dot(h, wg_ref[...], preferred_element_type=F32)
    u = jnp.dot(h, wu_ref[...], preferred_element_type=F32)
    act = (_silu(a) * u).astype(BF16)
    gate = 0.5 * gt_ref[...]
    for n in range(D_MODEL // COLS_DOWN):
        sl = slice(n * COLS_DOWN, (n + 1) * COLS_DOWN)
        o_ref[:, sl] += gate[:, sl] * jnp.dot(act, wd_ref[:, sl], preferred_element_type=F32)


def _ffn(x, mod5, gain, layer, jbase, wg, wu, wd):
    tm, tf = TM_FFN, TF_FFN
    resident = pl.Buffered(1)
    return pl.pallas_call(
        _ffn_kernel,
        grid=(T_ALL // tm, D_FF // tf),
        in_specs=[
            pl.BlockSpec((tm, D_MODEL), lambda i, f: (i, 0), pipeline_mode=resident),
            pl.BlockSpec((1, D_MODEL), lambda i, f: (0, 0)),
            _mod_spec(layer, jbase, tm),
            _mod_spec(layer, jbase + 1, tm),
            _mod_spec(layer, jbase + 2, tm),
            pl.BlockSpec((None, D_MODEL, tf), lambda i, f: (layer, 0, f)),
            pl.BlockSpec((None, D_MODEL, tf), lambda i, f: (layer, 0, f)),
            pl.BlockSpec((None, tf, D_MODEL), lambda i, f: (layer, f, 0)),
        ],
        out_specs=pl.BlockSpec((tm, D_MODEL), lambda i, f: (i, 0), pipeline_mode=resident),
        out_shape=jax.ShapeDtypeStruct((T_ALL, D_MODEL), F32),
        scratch_shapes=[pltpu.VMEM((tm, D_MODEL), BF16)],
        compiler_params=_cparams(("parallel", "arbitrary")),
        name="ffn",
    )(x, gain.reshape(1, D_MODEL), mod5, mod5, mod5, wg, wu, wd)


def _normmod_kernel(x_ref, g_ref, sh_ref, sc_ref, o_ref):
    o_ref[...] = _norm_mod(x_ref[...], g_ref[...], sc_ref[...], sh_ref[...]).astype(BF16)


def _normmod(x, mod5, gain, layer, jbase):
    tm = TM_EW
    return pl.pallas_call(
        _normmod_kernel,
        grid=(T_ALL // tm,),
        in_specs=[
            pl.BlockSpec((tm, D_MODEL), lambda i: (i, 0)),
            pl.BlockSpec((1, D_MODEL), lambda i: (0, 0)),
            _mod_spec(layer, jbase, tm),
            _mod_spec(layer, jbase + 1, tm),
        ],
        out_specs=pl.BlockSpec((tm, D_MODEL), lambda i: (i, 0)),
        out_shape=jax.ShapeDtypeStruct((T_ALL, D_MODEL), BF16),
        compiler_params=_cparams(("parallel",)),
        name="normmod",
    )(x, gain.reshape(1, D_MODEL), mod5, mod5)


def _rmsnorm_kernel(x_ref, g_ref, o_ref):
    x = x_ref[...]
    ms = jnp.mean(x * x, axis=-1, keepdims=True)
    o_ref[...] = x * lax.rsqrt(ms + EPS) * g_ref[...]


def _final_norm(x, gain, row0, rows):
    tm = TM_EW
    blk0 = row0 // tm
    return pl.pallas_call(
        _rmsnorm_kernel,
        grid=(rows // tm,),
        in_specs=[pl.BlockSpec((tm, D_MODEL), lambda i: (blk0 + i, 0)),
                  pl.BlockSpec((1, D_MODEL), lambda i: (0, 0))],
        out_specs=pl.BlockSpec((tm, D_MODEL), lambda i: (i, 0)),
        out_shape=jax.ShapeDtypeStruct((rows, D_MODEL), F32),
        compiler_params=_cparams(("parallel",)),
        name="final_norm",
    )(x, gain.reshape(1, D_MODEL))


def _proj_kernel(a_ref, w_ref, o_ref, *, g, tw):
    acc = jnp.dot(a_ref[...], w_ref[...], preferred_element_type=F32)
    for s in range(g):
        o_ref[s] = acc[:, s * tw:(s + 1) * tw]


def _proj(h, w, tw, g):
    n = w.shape[1]
    tm = TM_MM
    return pl.pallas_call(
        functools.partial(_proj_kernel, g=g, tw=tw),
        grid=(T_ALL // tm, n // (g * tw)),
        in_specs=[pl.BlockSpec((tm, D_MODEL), lambda i, j: (i, 0)),
                  pl.BlockSpec((D_MODEL, g * tw), lambda i, j: (0, j))],
        out_specs=pl.BlockSpec((g, tm, tw), lambda i, j: (j, i, 0)),
        out_shape=jax.ShapeDtypeStruct((n // tw, T_ALL, tw), F32),
        compiler_params=_cparams(("parallel", "arbitrary")),
        name="proj",
    )(h, w)


def _s5disc_kernel(are_ref, aim_ref, ldt_ref, bre_ref, bim_ref, lre_ref, lim_ref, bbre_ref, bbim_ref):
    a_re = are_ref[...]
    a_im = aim_ref[...]
    dt = jnp.exp(ldt_ref[...])
    mag = jnp.exp(a_re * dt)
    lam_re = mag * jnp.cos(a_im * dt)
    lam_im = mag * jnp.sin(a_im * dt)
    den = a_re * a_re + a_im * a_im
    z_re = ((lam_re - 1.0) * a_re + lam_im * a_im) / den
    z_im = (lam_im * a_re - (lam_re - 1.0) * a_im) / den
    b_re = bre_ref[...]
    b_im = bim_ref[...]
    lre_ref[...] = lam_re
    lim_ref[...] = lam_im
    bbre_ref[...] = z_re * b_re - z_im * b_im
    bbim_ref[...] = z_re * b_im + z_im * b_re


def _s5_discretise(a_re, a_im, log_dt, b_re, b_im):
    shp = (2, S5_GROUPS, S5_CH, S5_P)
    n = 2 * S5_GROUPS * S5_CH
    rep = lambda t: jnp.broadcast_to(t[:, :, None, :], shp).reshape(n, S5_P)
    ldt = jnp.broadcast_to(log_dt[:, :, None, None], shp).reshape(n, S5_P)
    tr = lambda t: t.swapaxes(-1, -2).reshape(n, S5_P)
    lam_re, lam_im, bb_re, bb_im = pl.pallas_call(
        _s5disc_kernel,
        out_shape=[jax.ShapeDtypeStruct((n, S5_P), F32)] * 4,
        name="s5_disc",
    )(rep(a_re), rep(a_im), ldt, tr(b_re), tr(b_im))
    return (lam_re.reshape(shp)[:, :, 0], lam_im.reshape(shp)[:, :, 0],
            bb_re.reshape(shp), bb_im.reshape(shp))


def _s5_kernel(uf_ref, ub_ref, bbd_ref, cbd_ref, lam_ref, h0_ref, yf_ref, yb_ref, fin_ref,
               st_ref, bu_ref, x_ref):
    _item_init(pl.program_id(1), st_ref, h0_ref)
    chains = [(d, j) for d in range(2) for j in range(S5_NCB)]
    u_refs = (uf_ref, ub_ref)
    y_refs = (yf_ref, yb_ref)

    sub = lambda s: pl.ds(s, CHUNK, stride=S5_SUB)
    for d, j in chains:
        bu = jnp.dot(u_refs[d][j].astype(BF16), bbd_ref[d, j], preferred_element_type=F32)
        for part in range(2):
            for s in range(S5_SUB):
                col = part * S5_NB + s * LANE
                bu_ref[d, j, part, sub(s), :] = bu[:, col:col + LANE]
    lam = [(lam_ref[d, j, 0], lam_ref[d, j, 1]) for d, j in chains]

    def step(t, carry):
        new = []
        for (d, j), (lam_re, lam_im), (xr, xi) in zip(chains, lam, carry):
            tt = CHUNK - 1 - t if d == 1 else t
            rows = pl.ds(pl.multiple_of(tt * S5_SUB, S5_SUB), S5_SUB)
            nr = lam_re * xr - lam_im * xi + bu_ref[d, j, 0, rows, :]
            ni = lam_re * xi + lam_im * xr + bu_ref[d, j, 1, rows, :]
            x_ref[d, j, 0, rows, :] = nr
            x_ref[d, j, 1, rows, :] = ni
            new.append((nr, ni))
        return tuple(new)

    init = tuple((st_ref[d, j, 0], st_ref[d, j, 1]) for d, j in chains)
    fin = lax.fori_loop(0, CHUNK, step, init, unroll=8)
    for (d, j), (xr, xi) in zip(chains, fin):
        st_ref[d, j, 0] = xr
        st_ref[d, j, 1] = xi
        fin_ref[d, j, 0] = xr
        fin_ref[d, j, 1] = xi
    for d, j in chains:
        x = jnp.concatenate([x_ref[d, j, part, sub(s), :] for part in range(2) for s in range(S5_SUB)], axis=1)
        y_refs[d][j] = jnp.dot(x.astype(BF16), cbd_ref[d, j], preferred_element_type=F32)


def _s5_scan(pa, bbd, cbd, lam, cache):
    c, nb = CHUNK, S5_NCB
    tile = (2, S5_SUB, LANE)
    uspec = lambda rev: pl.BlockSpec((nb, c, MXU_W), lambda g, w: (PA_S5U // nb + g, _item_rowblk(w, rev), 0))
    yspec = lambda rev: pl.BlockSpec((nb, c, MXU_W), lambda g, w: (g, _item_rowblk(w, rev), 0))
    return pl.pallas_call(
        _s5_kernel,
        grid=(S5_CB // nb, N_ITEMS),
        in_specs=[
            uspec(False),
            uspec(True),
            pl.BlockSpec((2, nb, MXU_W, 2 * S5_NB), lambda g, w: (0, g, 0, 0)),
            pl.BlockSpec((2, nb, 2 * S5_NB, MXU_W), lambda g, w: (0, g, 0, 0)),
            pl.BlockSpec((2, nb) + tile, lambda g, w: (0, g, 0, 0, 0)),
            pl.BlockSpec((None, 2, nb) + tile, lambda g, w: (_item_cache(w), 0, g, 0, 0, 0)),
        ],
        out_specs=[
            yspec(False),
            yspec(True),
            pl.BlockSpec((None, 2, nb) + tile, lambda g, w: (_item_fin(w), 0, g, 0, 0, 0)),
        ],
        out_shape=[jax.ShapeDtypeStruct((S5_CB, T_ALL, MXU_W), F32),
                   jax.ShapeDtypeStruct((S5_CB, T_ALL, MXU_W), F32),
                   jax.ShapeDtypeStruct((BATCH + 1, 2, S5_CB) + tile, F32)],
        scratch_shapes=[pltpu.VMEM((2, nb) + tile, F32),
                        pltpu.VMEM((2, nb, 2, c * S5_SUB, LANE), F32),
                        pltpu.VMEM((2, nb, 2, c * S5_SUB, LANE), F32)],
        compiler_params=_cparams(("parallel", "arbitrary")),
        name="s5_scan",
    )(pa, pa, bbd, cbd, lam, cache)


def _gelu_tanh(x):
    return 0.5 * x * (1.0 + jnp.tanh(math.sqrt(2.0 / math.pi) * (x + 0.044715 * x * x * x)))


def _s5post_kernel(yf_ref, yb_ref, u_ref, d_ref, w_ref, b_ref, o_ref):
    ya = []
    for cb in range(S5_CB):
        ya.append(_gelu_tanh(yf_ref[cb] + yb_ref[cb] + d_ref[cb] * u_ref[cb]))
    z = b_ref[...]
    for cb in range(S5_CB):
        z = z + jnp.dot(ya[cb].astype(BF16), w_ref[cb * MXU_W:(cb + 1) * MXU_W, :],
                        preferred_element_type=F32)
    for cb in range(S5_CB):
        sl = slice(cb * MXU_W, (cb + 1) * MXU_W)
        o_ref[:, sl] = (ya[cb] * _sigmoid(z[:, sl])).astype(BF16)


def _s5_post(yf, yb, pa, skip, glu_w, glu_b):
    tm = TM_EW
    return pl.pallas_call(
        _s5post_kernel,
        grid=(T_ALL // tm,),
        in_specs=[
            pl.BlockSpec((S5_CB, tm, MXU_W), lambda i: (0, i, 0)),
            pl.BlockSpec((S5_CB, tm, MXU_W), lambda i: (0, i, 0)),
            pl.BlockSpec((S5_CB, tm, MXU_W), lambda i: (0, i, 0)),
            pl.BlockSpec((S5_CB, 1, MXU_W), lambda i: (0, 0, 0)),
            pl.BlockSpec((S5_WIDTH, S5_WIDTH), lambda i: (0, 0)),
            pl.BlockSpec((1, S5_WIDTH), lambda i: (0, 0)),
        ],
        out_specs=pl.BlockSpec((tm, S5_WIDTH), lambda i: (i, 0)),
        out_shape=jax.ShapeDtypeStruct((T_ALL, S5_WIDTH), BF16),
        compiler_params=_cparams(("parallel",)),
        name="s5_post",
    )(yf, yb, pa, skip, glu_w, glu_b)


def _level_matrix(c, rev):
    ri = lax.broadcasted_iota(jnp.int32, (c, c), 0)
    ci = lax.broadcasted_iota(jnp.int32, (c, c), 1)
    x = ri ^ ci
    lvl = jnp.full((c, c), -1, jnp.int32)
    for l in range(int(math.log2(c))):
        lvl = jnp.where((x >> l) != 0, l, lvl)
    before = (ci >= ri) if rev else (ci <= ri)
    return jnp.where(before, lvl, 99)


def _cross_add(base, tot, m, to_second):
    out = []
    for b0 in range(0, base.shape[0], 2 * m):
        lo, hi = slice(b0, b0 + m), slice(b0 + m, b0 + 2 * m)
        if to_second:
            out += [base[lo], base[hi] + tot[lo]]
        else:
            out += [base[lo] + tot[hi], base[hi]]
    return jnp.concatenate(out, axis=0)


def _gla_chunk(q, k, v, lg, st, lvlmat, rev):
    c, kk = q.shape
    half = c // 2
    top, bot = slice(0, half), slice(half, c)
    rowk = lax.broadcasted_iota(jnp.int32, (c, kk), 0)
    nt = (((1,), (1,)), ((), ()))
    p = lg
    r = jnp.zeros_like(lg)
    diag = jnp.sum(q * k, axis=-1, keepdims=True)
    a_in = [jnp.where(lvlmat == -1, diag[rows], 0.0) for rows in (top, bot)]
    a_cross = None
    nlev = int(math.log2(c))
    for lvl in range(nlev):
        m = 1 << lvl
        qs = (q * jnp.exp(p)).astype(BF16)
        ks = (k * jnp.exp(r)).astype(BF16)
        if lvl < nlev - 1:
            for i, rows in enumerate((top, bot)):
                al = lax.dot_general(qs[rows], ks[rows], nt, preferred_element_type=F32)
                a_in[i] = jnp.where(lvlmat == lvl, al, a_in[i])
        else:
            later, earlier = (top, bot) if rev else (bot, top)
            a_cross = lax.dot_general(qs[later], ks[earlier], nt, preferred_element_type=F32)
        tot = p + r
        if m % S5_SUB == 0:
            p = _cross_add(p, tot, m, to_second=not rev)
            r = _cross_add(r, tot, m, to_second=rev)
            continue
        hi = (rowk & m) != 0
        t_prev = pltpu.roll(tot, m, 0)
        t_next = pltpu.roll(tot, c - m, 0)
        if rev:
            p = p + jnp.where(hi, 0.0, t_next)
            r = r + jnp.where(hi, t_prev, 0.0)
        else:
            p = p + jnp.where(hi, t_prev, 0.0)
            r = r + jnp.where(hi, 0.0, t_next)
    qe = (q * jnp.exp(p)).astype(BF16)
    ke = (k * jnp.exp(r)).astype(BF16)
    vb = v.astype(BF16)
    o_in = [jnp.dot(a_in[i].astype(BF16), vb[rows], preferred_element_type=F32)
            for i, rows in enumerate((top, bot))]
    o_x = jnp.dot(a_cross.astype(BF16), vb[bot if rev else top], preferred_element_type=F32)
    o_in[0 if rev else 1] = o_in[0 if rev else 1] + o_x
    o = jnp.concatenate(o_in, axis=0)
    o = o + lax.dot_general(qe, st.astype(BF16), nt, preferred_element_type=F32)
    decay = jnp.exp((p + r)[0:1, :])
    st_new = st * decay + lax.dot_general(vb, ke, (((0,), (0,)), ((), ())), preferred_element_type=F32)
    return o, st_new


def _log_sigmoid(z):
    return jnp.minimum(z, 0.0) - jnp.log(1.0 + jnp.exp(-jnp.abs(z)))


def _level_matrix_once(lvl_ref, rev):
    @pl.when(pl.program_id(0) == 0)
    def _():
        lvl_ref[...] = _level_matrix(CHUNK // 2, rev)

    return lvl_ref[...]


def _gla_kernel(q_ref, k_ref, v_ref, lr_ref, w2_ref, b2_ref, h0_ref, o_ref, fin_ref, st_ref, lvl_ref, *, rev):
    _item_init(pl.program_id(0), st_ref, h0_ref)
    lr = lr_ref[...].astype(BF16)
    lvlmat = _level_matrix_once(lvl_ref, rev)

    def head(h, carry):
        z = jnp.dot(lr, w2_ref[h], preferred_element_type=F32) + b2_ref[h]
        lg = _log_sigmoid(z) * (1.0 / GLA_TAU)
        q = q_ref[h] * (GLA_DK ** -0.5)
        o, st = _gla_chunk(q, k_ref[h], v_ref[h], lg, st_ref[h], lvlmat, rev)
        o_ref[h] = o
        st_ref[h] = st
        fin_ref[h] = st
        return carry

    lax.fori_loop(0, GLA_HEADS, head, 0)


def _gla_scan(pa, p2, plr, w2p, b2p, cache, d):
    c = CHUNK
    rev = d == 1
    hq = GLA_HEADS
    return pl.pallas_call(
        functools.partial(_gla_kernel, rev=rev),
        grid=(N_ITEMS,),
        in_specs=[
            pl.BlockSpec((hq, c, GLA_DKP), lambda w: (PA_GQ // hq, _item_rowblk(w, rev), 0)),
            pl.BlockSpec((hq, c, GLA_DKP), lambda w: (PA_GK // hq, _item_rowblk(w, rev), 0)),
            pl.BlockSpec((hq, c, GLA_DV), lambda w: (0, _item_rowblk(w, rev), 0)),
            pl.BlockSpec((None, c, MXU_W), lambda w: (0, _item_rowblk(w, rev), 0)),
            pl.BlockSpec((None, hq, MXU_W, GLA_DKP), lambda w: (d, 0, 0, 0)),
            pl.BlockSpec((None, hq, 1, GLA_DKP), lambda w: (d, 0, 0, 0)),
            pl.BlockSpec((None, None, hq, GLA_DV, GLA_DKP), lambda w: (_item_cache(w), d, 0, 0, 0)),
        ],
        out_specs=[
            pl.BlockSpec((hq, c, GLA_DV), lambda w: (0, _item_rowblk(w, rev), 0)),
            pl.BlockSpec((None, hq, GLA_DV, GLA_DKP), lambda w: (_item_fin(w), 0, 0, 0)),
        ],
        out_shape=[jax.ShapeDtypeStruct((hq, T_ALL, GLA_DV), F32),
                   jax.ShapeDtypeStruct((BATCH + 1, hq, GLA_DV, GLA_DKP), F32)],
        scratch_shapes=[pltpu.VMEM((hq, GLA_DV, GLA_DKP), F32), pltpu.VMEM((c // 2, c // 2), jnp.int32)],
        compiler_params=_cparams(("arbitrary",)),
        name="gla_scan",
    )(pa, pa, p2, plr, w2p, b2p, cache)


def _hg_kernel(q_ref, z_ref, v_ref, lb_ref, h0_ref, o_ref, fin_ref, st_ref, lvl_ref, *, rev):
    _item_init(pl.program_id(0), st_ref, h0_ref)
    kk = HG_EXPAND
    lvlmat = _level_matrix_once(lvl_ref, rev)

    def pair(pr, carry):
        q2 = q_ref[pr] * (HG_EXPAND ** -0.5)
        v2 = v_ref[pr]
        lb = lb_ref[pr]
        f = lb + (1.0 - lb) * _sigmoid(z_ref[pr])
        lg2 = jnp.log(jnp.maximum(f, F_FLOOR))
        k2 = 1.0 - f
        outs = []
        for s in range(2):
            sl = slice(s * kk, (s + 1) * kk)
            h = 2 * pr + s
            o, st = _gla_chunk(q2[:, sl], k2[:, sl], v2[:, sl], lg2[:, sl], st_ref[h], lvlmat, rev)
            st_ref[h] = st
            fin_ref[h] = st
            outs.append(o)
        o_ref[pr] = jnp.concatenate(outs, axis=1)
        return carry

    lax.fori_loop(0, HG_HEADS // 2, pair, 0)


def _hg_scan(ph, lbp, cache, d):
    c = CHUNK
    rev = d == 1
    np_ = HG_HEADS // 2
    ispec = lambda slot: pl.BlockSpec((np_, c, MXU_W), lambda w: (slot // np_, _item_rowblk(w, rev), 0))
    return pl.pallas_call(
        functools.partial(_hg_kernel, rev=rev),
        grid=(N_ITEMS,),
        in_specs=[
            ispec(PH_Q),
            ispec(PH_Z + d * np_),
            ispec(PH_I),
            pl.BlockSpec((None, np_, 1, MXU_W), lambda w: (d, 0, 0, 0)),
            pl.BlockSpec((None, None, HG_HEADS, HG_DV, HG_EXPAND), lambda w: (_item_cache(w), d, 0, 0, 0)),
        ],
        out_specs=[
            pl.BlockSpec((np_, c, MXU_W), lambda w: (0, _item_rowblk(w, rev), 0)),
            pl.BlockSpec((None, HG_HEADS, HG_DV, HG_EXPAND), lambda w: (_item_fin(w), 0, 0, 0)),
        ],
        out_shape=[jax.ShapeDtypeStruct((np_, T_ALL, MXU_W), F32),
                   jax.ShapeDtypeStruct((BATCH + 1, HG_HEADS, HG_DV, HG_EXPAND), F32)],
        scratch_shapes=[pltpu.VMEM((HG_HEADS, HG_DV, HG_EXPAND), F32), pltpu.VMEM((c // 2, c // 2), jnp.int32)],
        compiler_params=_cparams(("arbitrary",)),
        name="hg_scan",
    )(ph, ph, ph, lbp, cache)


def _headpost_kernel(of_ref, ob_ref, gate_ref, gn_ref, out_ref, *, hp, v):
    gn = gn_ref[...]
    for s in range(hp):
        sl = slice(s * v, (s + 1) * v)
        o = of_ref[:, sl] + ob_ref[:, sl]
        y = o * lax.rsqrt(jnp.mean(o * o, axis=-1, keepdims=True) + EPS) * gn
        out_ref[:, sl] = (y * _silu(gate_ref[:, sl])).astype(BF16)


def _head_post(o_f, o_b, gate_src, gate_slot0, gain, hp, v):
    tm = TM_HEAD
    nh, _, w = o_f.shape
    return pl.pallas_call(
        functools.partial(_headpost_kernel, hp=hp, v=v),
        grid=(T_ALL // tm, nh),
        in_specs=[
            pl.BlockSpec((None, tm, w), lambda i, j: (j, i, 0)),
            pl.BlockSpec((None, tm, w), lambda i, j: (j, i, 0)),
            pl.BlockSpec((None, tm, w), lambda i, j: (gate_slot0 + j, i, 0)),
            pl.BlockSpec((1, v), lambda i, j: (0, 0)),
        ],
        out_specs=pl.BlockSpec((tm, w), lambda i, j: (i, j)),
        out_shape=jax.ShapeDtypeStruct((T_ALL, nh * w), BF16),
        compiler_params=_cparams(("parallel", "parallel")),
        name="head_post",
    )(o_f, o_b, gate_src, gain.reshape(1, v))


def _wout_kernel(a_ref, b_ref, c_ref, wa_ref, wb_ref, wc_ref, x_ref, gt_ref, o_ref):
    acc = jnp.dot(a_ref[...], wa_ref[...], preferred_element_type=F32)
    acc = acc + jnp.dot(b_ref[...], wb_ref[...], preferred_element_type=F32)
    acc = acc + jnp.dot(c_ref[...], wc_ref[...], preferred_element_type=F32)
    o_ref[...] = x_ref[...] + gt_ref[...] * acc


def _wout(ma, mb, mc, wa, wb, wc, x, mod5, layer, jgate):
    tm, tn = TM_WOUT, 1024
    return pl.pallas_call(
        _wout_kernel,
        grid=(T_ALL // tm, D_MODEL // tn),
        in_specs=[
            pl.BlockSpec((tm, S5_WIDTH), lambda i, j: (i, 0)),
            pl.BlockSpec((tm, GLA_WIDTH), lambda i, j: (i, 0)),
            pl.BlockSpec((tm, HG_WIDTH), lambda i, j: (i, 0)),
            pl.BlockSpec((S5_WIDTH, tn), lambda i, j: (0, j)),
            pl.BlockSpec((GLA_WIDTH, tn), lambda i, j: (0, j)),
            pl.BlockSpec((HG_WIDTH, tn), lambda i, j: (0, j)),
            pl.BlockSpec((tm, tn), lambda i, j: (i, j)),
            pl.BlockSpec((None, None, None, 1, tn), lambda i, j: (layer, _mod_row(i, tm), jgate, 0, j)),
        ],
        out_specs=pl.BlockSpec((tm, tn), lambda i, j: (i, j)),
        out_shape=jax.ShapeDtypeStruct((T_ALL, D_MODEL), F32),
        compiler_params=_cparams(("parallel", "arbitrary")),
        name="wout",
    )(ma, mb, mc, wa, wb, wc, x, mod5)


def _pad_heads(w, nh, dk, dkp):
    return jnp.pad(w.reshape(w.shape[0], nh, dk), ((0, 0), (0, 0), (0, dkp - dk))).reshape(w.shape[0], nh * dkp)


def _split_w_in(w):
    offs = [0]
    for s in (S5_WIDTH, GLA_HEADS * GLA_DK, GLA_HEADS * GLA_DK, GLA_WIDTH, GLA_WIDTH, 2 * GLA_RANK,
              HG_WIDTH, HG_WIDTH, HG_WIDTH, HG_WIDTH, HG_WIDTH):
        offs.append(offs[-1] + s)
    (u_a, q_b, k_b, v_b, g_b, lr_b, q_c, zf_c, zb_c, i_c, g_c) = [w[:, offs[i]:offs[i + 1]] for i in range(11)]
    wa = jnp.concatenate([
        u_a,
        _pad_heads(q_b, GLA_HEADS, GLA_DK, GLA_DKP),
        _pad_heads(k_b, GLA_HEADS, GLA_DK, GLA_DKP),
    ], axis=1).astype(BF16)
    wlr = jnp.pad(lr_b, ((0, 0), (0, MXU_W - 2 * GLA_RANK))).astype(BF16)
    wh = jnp.concatenate([q_c, zf_c, zb_c, i_c, g_c], axis=1).astype(BF16)
    w2 = jnp.concatenate([v_b, g_b], axis=1).astype(BF16)
    return wa, wlr, wh, w2


def _lat_grid_transpose(t):
    w = t.shape[1]
    rows = DEC_SEQ // GRID_W
    lat = t[T_CTX:].reshape(DEC_BATCH, rows, GRID_W, w).swapaxes(1, 2).reshape(T_LAT, w)
    return jnp.concatenate([t[:T_CTX], lat], axis=0)


def _s5_blockdiag(bb_re, bb_im, c_re, c_im):
    eye = jnp.eye(S5_GB, dtype=F32)

    def bmat(bb):
        t = bb.reshape(2, S5_CB, S5_GB, S5_CH, S5_P)
        return jnp.einsum('dbgcp,gh->dbgchp', t, eye).reshape(2, S5_CB, S5_GB * S5_CH, S5_NB)

    def cmat(cc):
        t = cc.reshape(2, S5_CB, S5_GB, S5_CH, S5_P)
        return jnp.einsum('dbgcp,gh->dbgphc', t, eye).reshape(2, S5_CB, S5_NB, S5_GB * S5_CH)

    bbd = jnp.concatenate([bmat(bb_re), bmat(bb_im)], axis=3).astype(BF16)
    cbd = jnp.concatenate([cmat(c_re), -cmat(c_im)], axis=2).astype(BF16)
    return bbd, cbd


def _s5_state_in(re, im):
    lead = re.shape[:-2]
    f = lambda t: t.reshape(lead + (S5_CB, 1, S5_SUB, LANE))
    return jnp.concatenate([f(re), f(im)], axis=-3)


def kernel(x_prompt, x_sample, state_s5_re, state_s5_im, state_gla, state_hgrn, c, c_ctx, ada_w, ada_b, norm_g, ffn1_wg, ffn1_wu, ffn1_wd, ffn2_wg, ffn2_wu, ffn2_wd, w_in, w_out, s5_a_re, s5_a_im, s5_log_dt, s5_b_re, s5_b_im, s5_c_re, s5_c_im, s5_d, s5_glu_w, s5_glu_b, gla_w2, gla_b2, gla_norm_g, hg_lb_raw, hg_norm_g, final_norm_g):
    x = jnp.concatenate([x_prompt.reshape(T_CTX, D_MODEL), x_sample.reshape(T_LAT, D_MODEL)], axis=0)
    cond8 = jnp.zeros((N_COND, D_MODEL), F32).at[0].set(c_ctx).at[1:1 + DEC_BATCH].set(c)
    mod5 = _ada(cond8, ada_w, ada_b).reshape(DEPTH, N_COND, N_MOD, 1, D_MODEL)
    hg_lb = _hg_lb(hg_lb_raw.astype(F32).swapaxes(0, 1))

    ffn1_w = (ffn1_wg.astype(BF16), ffn1_wu.astype(BF16), ffn1_wd.astype(BF16))
    ffn2_w = (ffn2_wg.astype(BF16), ffn2_wu.astype(BF16), ffn2_wd.astype(BF16))

    new_re, new_im, new_gla, new_hg = [], [], [], []
    for l in range(DEPTH):
        x = _ffn(x, mod5, norm_g[l, 0], l, 0, *ffn1_w)

        h = _normmod(x, mod5, norm_g[l, 1], l, 3)
        wa, wlr, wh, w2 = _split_w_in(w_in[l])
        pa = _proj(h, wa, MXU_W, 4)
        plr = _proj(h, wlr, MXU_W, 1)
        p2 = _proj(h, w2, GLA_DV, 2)
        ph = _proj(_lat_grid_transpose(h), wh, MXU_W, 3)

        lam_re, lam_im, bb_re, bb_im = _s5_discretise(s5_a_re[l], s5_a_im[l], s5_log_dt[l], s5_b_re[l], s5_b_im[l])
        bbd, cbd = _s5_blockdiag(bb_re, bb_im, s5_c_re[l].astype(F32), s5_c_im[l].astype(F32))
        lam = _s5_state_in(lam_re, lam_im)
        s5_cache = _s5_state_in(state_s5_re[:, l], state_s5_im[:, l])
        y_f, y_b, s5_fin = _s5_scan(pa, bbd, cbd, lam, s5_cache)
        new_re.append(s5_fin[:BATCH, :, :, 0].reshape(BATCH, 2, S5_GROUPS, S5_P))
        new_im.append(s5_fin[:BATCH, :, :, 1].reshape(BATCH, 2, S5_GROUPS, S5_P))
        out_a = _s5_post(y_f, y_b, pa, s5_d[l].reshape(S5_CB, 1, MXU_W),
                         s5_glu_w[l].astype(BF16), s5_glu_b[l].reshape(1, S5_WIDTH))

        w2p = jnp.zeros((2, MXU_W, GLA_HEADS, GLA_DKP), F32)
        for d in range(2):
            w2p = w2p.at[d, d * GLA_RANK:(d + 1) * GLA_RANK, :, :GLA_DK].set(
                gla_w2[l, d].reshape(GLA_RANK, GLA_HEADS, GLA_DK))
        w2p = w2p.transpose(0, 2, 1, 3).astype(BF16)
        b2p = jnp.pad(gla_b2[l].reshape(2, GLA_HEADS, 1, GLA_DK), ((0, 0), (0, 0), (0, 0), (0, GLA_DKP - GLA_DK)))
        kpad = ((0, 0),) * 4 + ((0, GLA_DKP - GLA_DK),)
        gla_cache = jnp.pad(state_gla[:, l].swapaxes(-1, -2), kpad)
        os_, gf = [], []
        for d in range(2):
            o_d, fin_d = _gla_scan(pa, p2, plr, w2p, b2p, gla_cache, d)
            os_.append(o_d)
            gf.append(fin_d[:BATCH, ..., :GLA_DK].swapaxes(-1, -2))
        new_gla.append(jnp.stack(gf, axis=1))
        out_b = _head_post(os_[0], os_[1], p2, GLA_HEADS, gla_norm_g[l], 1, GLA_DV)

        lbp = hg_lb[l].reshape(2, HG_HEADS // 2, 1, MXU_W)
        hg_cache = state_hgrn[:, l].swapaxes(-1, -2)
        os_, hf = [], []
        for d in range(2):
            o_d, fin_d = _hg_scan(ph, lbp, hg_cache, d)
            os_.append(o_d)
            hf.append(fin_d[:BATCH].swapaxes(-1, -2))
        new_hg.append(jnp.stack(hf, axis=1))
        out_c = _lat_grid_transpose(_head_post(os_[0], os_[1], ph, PH_G, hg_norm_g[l], 2, HG_DV))

        wo = w_out[l]
        x = _wout(out_a, out_b, out_c, wo[:S5_WIDTH].astype(BF16),
                  wo[S5_WIDTH:S5_WIDTH + GLA_WIDTH].astype(BF16), wo[S5_WIDTH + GLA_WIDTH:].astype(BF16),
                  x, mod5, l, 5)

        x = _ffn(x, mod5, norm_g[l, 2], l, 6, *ffn2_w)

    y_prompt = _final_norm(x, final_norm_g, 0, T_CTX).reshape(BATCH, SEQ, D_MODEL)
    y_sample = _final_norm(x, final_norm_g, T_CTX, T_LAT).reshape(DEC_BATCH, DEC_SEQ, D_MODEL)
    return (y_prompt, y_sample, jnp.stack(new_re, axis=1), jnp.stack(new_im, axis=1),
            jnp.stack(new_gla, axis=1), jnp.stack(new_hg, axis=1))
```

```python
import functools
import math

import jax
import jax.numpy as jnp
from jax import lax
from jax.experimental import pallas as pl
from jax.experimental.pallas import tpu as pltpu

F32 = jnp.float32
BF16 = jnp.bfloat16

D_MODEL = 4096
BATCH = 32
SEQ = 256
DEPTH = 2
DEC_BATCH = 2
DEC_SEQ = 4096
GRID_W = 64
S5_WIDTH = 1024
S5_CH = 16
S5_GROUPS = 64
S5_P = 64
GLA_WIDTH = 1536
GLA_HEADS = 4
GLA_DK = 192
GLA_DV = 384
GLA_RANK = 16
GLA_TAU = 16.0
HG_WIDTH = 1536
HG_EXPAND = 128
HG_HEADS = 12
HG_DV = 128
D_FF = 11008
N_MOD = 9
EPS = 1e-6
F_FLOOR = 1e-30

T_CTX = BATCH * SEQ
T_LAT = DEC_BATCH * DEC_SEQ
T_ALL = T_CTX + T_LAT
N_COND = 8
LANE = 128
S5_SUB = 8
MXU_W = 256
GLA_DKP = 256
S5_CB = 4
S5_GB = S5_GROUPS // S5_CB
S5_NB = S5_GB * S5_P
S5_NCB = 2
CHUNK = 256
HG_COLS = 8
HG_LAT_ITEMS = GRID_W // HG_COLS
LAT_CHUNKS = DEC_SEQ // CHUNK
N_ITEMS = BATCH + DEC_BATCH * LAT_CHUNKS
VMEM_LIMIT = 56 * 1024 * 1024
assert SEQ == CHUNK and T_CTX % DEC_SEQ == 0

TM_FFN = 1024
TF_FFN = 256
TM_MM = 1024
TM_WOUT = 512
TM_EW = 256
TM_HEAD = 1024
ROWS_EW = 64
COLS_DOWN = 512

PA_S5U = 0
PA_GQ = 4
PA_GK = 8
PA_SLOTS = 12
PH_Q = 0
PH_Z = 6
PH_I = 18
PH_G = 24
PH_SLOTS = 30


def _cparams(sem):
    return pltpu.CompilerParams(dimension_semantics=sem, vmem_limit_bytes=VMEM_LIMIT)


def _mod_row(i, tm):
    start = i * tm
    return jnp.where(start < T_CTX, 0, 1 + (start - T_CTX) // DEC_SEQ)


def _mod_spec(layer, j, tm):
    return pl.BlockSpec((None, None, None, 1, D_MODEL),
                        lambda i, *_: (layer, _mod_row(i, tm), j, 0, 0))


def _item_rowblk(w, rev):
    j = jnp.maximum(w - BATCH, 0)
    c = j % LAT_CHUNKS
    if rev:
        c = LAT_CHUNKS - 1 - c
    return jnp.where(w < BATCH, w, BATCH + (j // LAT_CHUNKS) * LAT_CHUNKS + c)


def _item_cache(w):
    return jnp.maximum(w - BATCH, 0) // LAT_CHUNKS


def _item_fin(w):
    return jnp.minimum(w, BATCH)


def _item_init(w, st_ref, h0_ref):
    @pl.when(w < BATCH)
    def _():
        st_ref[...] = jnp.zeros(st_ref.shape, st_ref.dtype)

    @pl.when(jnp.logical_and(w >= BATCH, (w - BATCH) % LAT_CHUNKS == 0))
    def _():
        st_ref[...] = h0_ref[...]


def _sigmoid(x):
    return 1.0 / (1.0 + jnp.exp(-x))


def _silu(x):
    return x * _sigmoid(x)


def _norm_mod(x, g, sc, sh):
    ms = jnp.mean(x * x, axis=-1, keepdims=True)
    y = x * lax.rsqrt(ms + EPS) * g
    return y * (1.0 + sc) + sh


def _ada_kernel(c_ref, w_ref, b_ref, o_ref):
    s = _silu(c_ref[...]).astype(BF16)
    o_ref[...] = jnp.dot(s, w_ref[...].astype(BF16), preferred_element_type=F32) + b_ref[...]


def _ada(cond8, ada_w, ada_b):
    tn = 512
    n = N_MOD * D_MODEL
    return pl.pallas_call(
        _ada_kernel,
        grid=(DEPTH, n // tn),
        in_specs=[
            pl.BlockSpec((N_COND, D_MODEL), lambda l, j: (0, 0)),
            pl.BlockSpec((None, D_MODEL, tn), lambda l, j: (l, 0, j)),
            pl.BlockSpec((None, 1, tn), lambda l, j: (l, 0, j)),
        ],
        out_specs=pl.BlockSpec((None, N_COND, tn), lambda l, j: (l, 0, j)),
        out_shape=jax.ShapeDtypeStruct((DEPTH, N_COND, n), F32),
        compiler_params=_cparams(("parallel", "parallel")),
        name="ada",
    )(cond8, ada_w, ada_b.reshape(DEPTH, 1, n))


def _lb_kernel(raw_ref, o_ref):
    rows = [raw_ref[l] for l in range(DEPTH)]
    mx = functools.reduce(jnp.maximum, rows)
    es = [jnp.exp(r - mx) for r in rows]
    den = functools.reduce(lambda a, b: a + b, es)
    ps = [e / den for e in es]
    acc = ps[0]
    o_ref[0] = acc - ps[0]
    for l in range(1, DEPTH):
        acc = acc + ps[l]
        o_ref[l] = acc - ps[0]


def _hg_lb(raw):
    return pl.pallas_call(
        _lb_kernel,
        out_shape=jax.ShapeDtypeStruct(raw.shape, F32),
        name="hg_lb",
    )(raw)


def _ffn_kernel(x_ref, g_ref, sh_ref, sc_ref, gt_ref, wg_ref, wu_ref, wd_ref, o_ref, h_ref):
    @pl.when(pl.program_id(1) == 0)
    def _():
        def rows(r, carry):
            sl = pl.ds(pl.multiple_of(r * ROWS_EW, ROWS_EW), ROWS_EW)
            x = x_ref[sl, :]
            h_ref[sl, :] = _norm_mod(x, g_ref[...], sc_ref[...], sh_ref[...]).astype(BF16)
            o_ref[sl, :] = x
            return carry

        lax.fori_loop(0, x_ref.shape[0] // ROWS_EW, rows, 0)

    h = h_ref[...]
    a = jnp.dot(h, wg_ref[...], preferred_element_type=F32)
    u = jnp.dot(h, wu_ref[...], preferred_element_type=F32)
    act = (_silu(a) * u).astype(BF16)
    gate = 0.5 * gt_ref[...]
    for n in range(D_MODEL // COLS_DOWN):
        sl = slice(n * COLS_DOWN, (n + 1) * COLS_DOWN)
        o_ref[:, sl] += gate[:, sl] * jnp.dot(act, wd_ref[:, sl], preferred_element_type=F32)


def _ffn(x, mod5, gain, layer, jbase, wg, wu, wd):
    tm, tf = TM_FFN, TF_FFN
    resident = pl.Buffered(1)
    return pl.pallas_call(
        _ffn_kernel,
        grid=(T_ALL // tm, D_FF // tf),
        in_specs=[
            pl.BlockSpec((tm, D_MODEL), lambda i, f: (i, 0), pipeline_mode=resident),
            pl.BlockSpec((1, D_MODEL), lambda i, f: (0, 0)),
            _mod_spec(layer, jbase, tm),
            _mod_spec(layer, jbase + 1, tm),
            _mod_spec(layer, jbase + 2, tm),
            pl.BlockSpec((None, D_MODEL, tf), lambda i, f: (layer, 0, f)),
            pl.BlockSpec((None, D_MODEL, tf), lambda i, f: (layer, 0, f)),
            pl.BlockSpec((None, tf, D_MODEL), lambda i, f: (layer, f, 0)),
        ],
        out_specs=pl.BlockSpec((tm, D_MODEL), lambda i, f: (i, 0), pipeline_mode=resident),
        out_shape=jax.ShapeDtypeStruct((T_ALL, D_MODEL), F32),
        scratch_shapes=[pltpu.VMEM((tm, D_MODEL), BF16)],
        compiler_params=_cparams(("parallel", "arbitrary")),
        name="ffn",
    )(x, gain.reshape(1, D_MODEL), mod5, mod5, mod5, wg, wu, wd)


def _normmod_kernel(x_ref, g_ref, sh_ref, sc_ref, o_ref):
    o_ref[...] = _norm_mod(x_ref[...], g_ref[...], sc_ref[...], sh_ref[...]).astype(BF16)


def _normmod(x, mod5, gain, layer, jbase):
    tm = TM_EW
    return pl.pallas_call(
        _normmod_kernel,
        grid=(T_ALL // tm,),
        in_specs=[
            pl.BlockSpec((tm, D_MODEL), lambda i: (i, 0)),
            pl.BlockSpec((1, D_MODEL), lambda i: (0, 0)),
            _mod_spec(layer, jbase, tm),
            _mod_spec(layer, jbase + 1, tm),
        ],
        out_specs=pl.BlockSpec((tm, D_MODEL), lambda i: (i, 0)),
        out_shape=jax.ShapeDtypeStruct((T_ALL, D_MODEL), BF16),
        compiler_params=_cparams(("parallel",)),
        name="normmod",
    )(x, gain.reshape(1, D_MODEL), mod5, mod5)


def _rmsnorm_kernel(x_ref, g_ref, o_ref):
    x = x_ref[...]
    ms = jnp.mean(x * x, axis=-1, keepdims=True)
    o_ref[...] = x * lax.rsqrt(ms + EPS) * g_ref[...]


def _final_norm(x, gain, row0, rows):
    tm = TM_EW
    blk0 = row0 // tm
    return pl.pallas_call(
        _rmsnorm_kernel,
        grid=(rows // tm,),
        in_specs=[pl.BlockSpec((tm, D_MODEL), lambda i: (blk0 + i, 0)),
                  pl.BlockSpec((1, D_MODEL), lambda i: (0, 0))],
        out_specs=pl.BlockSpec((tm, D_MODEL), lambda i: (i, 0)),
        out_shape=jax.ShapeDtypeStruct((rows, D_MODEL), F32),
        compiler_params=_cparams(("parallel",)),
        name="final_norm",
    )(x, gain.reshape(1, D_MODEL))


def _proj_kernel(a_ref, w_ref, o_ref, *, g, tw):
    acc = jnp.dot(a_ref[...], w_ref[...], preferred_element_type=F32)
    for s in range(g):
        o_ref[s] = acc[:, s * tw:(s + 1) * tw]


def _proj(h, w, layer, tw, g):
    n = w.shape[2]
    tm = TM_MM
    return pl.pallas_call(
        functools.partial(_proj_kernel, g=g, tw=tw),
        grid=(T_ALL // tm, n // (g * tw)),
        in_specs=[pl.BlockSpec((tm, D_MODEL), lambda i, j: (i, 0)),
                  pl.BlockSpec((None, D_MODEL, g * tw), lambda i, j: (layer, 0, j))],
        out_specs=pl.BlockSpec((g, tm, tw), lambda i, j: (j, i, 0)),
        out_shape=jax.ShapeDtypeStruct((n // tw, T_ALL, tw), F32),
        compiler_params=_cparams(("parallel", "arbitrary")),
        name="proj",
    )(h, w)


def _s5disc_kernel(are_ref, aim_ref, ldt_ref, bre_ref, bim_ref, lre_ref, lim_ref, bbre_ref, bbim_ref):
    a_re = are_ref[...]
    a_im = aim_ref[...]
    dt = jnp.exp(ldt_ref[...])
    mag = jnp.exp(a_re * dt)
    lam_re = mag * jnp.cos(a_im * dt)
    lam_im = mag * jnp.sin(a_im * dt)
    den = a_re * a_re + a_im * a_im
    z_re = ((lam_re - 1.0) * a_re + lam_im * a_im) / den
    z_im = (lam_im * a_re - (lam_re - 1.0) * a_im) / den
    b_re = bre_ref[...]
    b_im = bim_ref[...]
    lre_ref[...] = lam_re
    lim_ref[...] = lam_im
    bbre_ref[...] = z_re * b_re - z_im * b_im
    bbim_ref[...] = z_re * b_im + z_im * b_re


def _s5_discretise(a_re, a_im, log_dt, b_re, b_im):
    shp = (2, S5_GROUPS, S5_CH, S5_P)
    n = 2 * S5_GROUPS * S5_CH
    rep = lambda t: jnp.broadcast_to(t[:, :, None, :], shp).reshape(n, S5_P)
    ldt = jnp.broadcast_to(log_dt[:, :, None, None], shp).reshape(n, S5_P)
    tr = lambda t: t.swapaxes(-1, -2).reshape(n, S5_P)
    lam_re, lam_im, bb_re, bb_im = pl.pallas_call(
        _s5disc_kernel,
        out_shape=[jax.ShapeDtypeStruct((n, S5_P), F32)] * 4,
        name="s5_disc",
    )(rep(a_re), rep(a_im), ldt, tr(b_re), tr(b_im))
    return (lam_re.reshape(shp)[:, :, 0], lam_im.reshape(shp)[:, :, 0],
            bb_re.reshape(shp), bb_im.reshape(shp))


def _s5_kernel(uf_ref, ub_ref, bbd_ref, cbd_ref, lam_ref, h0_ref, yf_ref, yb_ref, fin_ref,
               st_ref, bu_ref, x_ref):
    _item_init(pl.program_id(1), st_ref, h0_ref)
    chains = [(d, j) for d in range(2) for j in range(S5_NCB)]
    u_refs = (uf_ref, ub_ref)
    y_refs = (yf_ref, yb_ref)

    sub = lambda s: pl.ds(s, CHUNK, stride=S5_SUB)
    for d, j in chains:
        bu = jnp.dot(u_refs[d][j].astype(BF16), bbd_ref[d, j], preferred_element_type=F32)
        for part in range(2):
            for s in range(S5_SUB):
                col = part * S5_NB + s * LANE
                bu_ref[d, j, part, sub(s), :] = bu[:, col:col + LANE]
    lam = [(lam_ref[d, j, 0], lam_ref[d, j, 1]) for d, j in chains]

    def step(t, carry):
        new = []
        for (d, j), (lam_re, lam_im), (xr, xi) in zip(chains, lam, carry):
            tt = CHUNK - 1 - t if d == 1 else t
            rows = pl.ds(pl.multiple_of(tt * S5_SUB, S5_SUB), S5_SUB)
            nr = lam_re * xr - lam_im * xi + bu_ref[d, j, 0, rows, :]
            ni = lam_re * xi + lam_im * xr + bu_ref[d, j, 1, rows, :]
            x_ref[d, j, 0, rows, :] = nr
            x_ref[d, j, 1, rows, :] = ni
            new.append((nr, ni))
        return tuple(new)

    init = tuple((st_ref[d, j, 0], st_ref[d, j, 1]) for d, j in chains)
    fin = lax.fori_loop(0, CHUNK, step, init, unroll=8)
    for (d, j), (xr, xi) in zip(chains, fin):
        st_ref[d, j, 0] = xr
        st_ref[d, j, 1] = xi
        fin_ref[d, j, 0] = xr
        fin_ref[d, j, 1] = xi
    for d, j in chains:
        x = jnp.concatenate([x_ref[d, j, part, sub(s), :] for part in range(2) for s in range(S5_SUB)], axis=1)
        y_refs[d][j] = jnp.dot(x.astype(BF16), cbd_ref[d, j], preferred_element_type=F32)


def _s5_scan(pa, bbd, cbd, lam, cache):
    c, nb = CHUNK, S5_NCB
    tile = (2, S5_SUB, LANE)
    uspec = lambda rev: pl.BlockSpec((nb, c, MXU_W), lambda g, w: (PA_S5U // nb + g, _item_rowblk(w, rev), 0))
    yspec = lambda rev: pl.BlockSpec((nb, c, MXU_W), lambda g, w: (g, _item_rowblk(w, rev), 0))
    return pl.pallas_call(
        _s5_kernel,
        grid=(S5_CB // nb, N_ITEMS),
        in_specs=[
            uspec(False),
            uspec(True),
            pl.BlockSpec((2, nb, MXU_W, 2 * S5_NB), lambda g, w: (0, g, 0, 0)),
            pl.BlockSpec((2, nb, 2 * S5_NB, MXU_W), lambda g, w: (0, g, 0, 0)),
            pl.BlockSpec((2, nb) + tile, lambda g, w: (0, g, 0, 0, 0)),
            pl.BlockSpec((None, 2, nb) + tile, lambda g, w: (_item_cache(w), 0, g, 0, 0, 0)),
        ],
        out_specs=[
            yspec(False),
            yspec(True),
            pl.BlockSpec((None, 2, nb) + tile, lambda g, w: (_item_fin(w), 0, g, 0, 0, 0)),
        ],
        out_shape=[jax.ShapeDtypeStruct((S5_CB, T_ALL, MXU_W), F32),
                   jax.ShapeDtypeStruct((S5_CB, T_ALL, MXU_W), F32),
                   jax.ShapeDtypeStruct((BATCH + 1, 2, S5_CB) + tile, F32)],
        scratch_shapes=[pltpu.VMEM((2, nb) + tile, F32),
                        pltpu.VMEM((2, nb, 2, c * S5_SUB, LANE), F32),
                        pltpu.VMEM((2, nb, 2, c * S5_SUB, LANE), F32)],
        compiler_params=_cparams(("parallel", "arbitrary")),
        name="s5_scan",
    )(pa, pa, bbd, cbd, lam, cache)


def _gelu_tanh(x):
    return 0.5 * x * (1.0 + jnp.tanh(math.sqrt(2.0 / math.pi) * (x + 0.044715 * x * x * x)))


def _s5post_kernel(yf_ref, yb_ref, u_ref, d_ref, w_ref, b_ref, o_ref):
    ya = []
    for cb in range(S5_CB):
        ya.append(_gelu_tanh(yf_ref[cb] + yb_ref[cb] + d_ref[cb] * u_ref[cb]))
    z = b_ref[...]
    for cb in range(S5_CB):
        z = z + jnp.dot(ya[cb].astype(BF16), w_ref[cb * MXU_W:(cb + 1) * MXU_W, :],
                        preferred_element_type=F32)
    for cb in range(S5_CB):
        sl = slice(cb * MXU_W, (cb + 1) * MXU_W)
        o_ref[:, sl] = (ya[cb] * _sigmoid(z[:, sl])).astype(BF16)


def _s5_post(yf, yb, pa, skip, glu_w, glu_b):
    tm = TM_EW
    return pl.pallas_call(
        _s5post_kernel,
        grid=(T_ALL // tm,),
        in_specs=[
            pl.BlockSpec((S5_CB, tm, MXU_W), lambda i: (0, i, 0)),
            pl.BlockSpec((S5_CB, tm, MXU_W), lambda i: (0, i, 0)),
            pl.BlockSpec((S5_CB, tm, MXU_W), lambda i: (0, i, 0)),
            pl.BlockSpec((S5_CB, 1, MXU_W), lambda i: (0, 0, 0)),
            pl.BlockSpec((S5_WIDTH, S5_WIDTH), lambda i: (0, 0)),
            pl.BlockSpec((1, S5_WIDTH), lambda i: (0, 0)),
        ],
        out_specs=pl.BlockSpec((tm, S5_WIDTH), lambda i: (i, 0)),
        out_shape=jax.ShapeDtypeStruct((T_ALL, S5_WIDTH), BF16),
        compiler_params=_cparams(("parallel",)),
        name="s5_post",
    )(yf, yb, pa, skip, glu_w, glu_b)


def _level_matrix(c, rev):
    ri = lax.broadcasted_iota(jnp.int32, (c, c), 0)
    ci = lax.broadcasted_iota(jnp.int32, (c, c), 1)
    x = ri ^ ci
    lvl = jnp.full((c, c), -1, jnp.int32)
    for l in range(int(math.log2(c))):
        lvl = jnp.where((x >> l) != 0, l, lvl)
    before = (ci >= ri) if rev else (ci <= ri)
    return jnp.where(before, lvl, 99)


def _cross_add(base, tot, m, to_second):
    out = []
    for b0 in range(0, base.shape[0], 2 * m):
        lo, hi = slice(b0, b0 + m), slice(b0 + m, b0 + 2 * m)
        if to_second:
            out += [base[lo], base[hi] + tot[lo]]
        else:
            out += [base[lo] + tot[hi], base[hi]]
    return jnp.concatenate(out, axis=0)


def _gla_chunk(q, k, v, lg, st, lvlmat, rev):
    c, kk = q.shape
    half = c // 2
    top, bot = slice(0, half), slice(half, c)
    rowk = lax.broadcasted_iota(jnp.int32, (c, kk), 0)
    nt = (((1,), (1,)), ((), ()))
    p = lg
    r = jnp.zeros_like(lg)
    diag = jnp.sum(q * k, axis=-1, keepdims=True)
    a_in = [jnp.where(lvlmat == -1, diag[rows], 0.0) for rows in (top, bot)]
    a_cross = None
    nlev = int(math.log2(c))
    for lvl in range(nlev):
        m = 1 << lvl
        qs = (q * jnp.exp(p)).astype(BF16)
        ks = (k * jnp.exp(r)).astype(BF16)
        if lvl < nlev - 1:
            for i, rows in enumerate((top, bot)):
                al = lax.dot_general(qs[rows], ks[rows], nt, preferred_element_type=F32)
                a_in[i] = jnp.where(lvlmat == lvl, al, a_in[i])
        else:
            later, earlier = (top, bot) if rev else (bot, top)
            a_cross = lax.dot_general(qs[later], ks[earlier], nt, preferred_element_type=F32)
        tot = p + r
        if m % S5_SUB == 0:
            p = _cross_add(p, tot, m, to_second=not rev)
            r = _cross_add(r, tot, m, to_second=rev)
            continue
        hi = (rowk & m) != 0
        t_prev = pltpu.roll(tot, m, 0)
        t_next = pltpu.roll(tot, c - m, 0)
        if rev:
            p = p + jnp.where(hi, 0.0, t_next)
            r = r + jnp.where(hi, t_prev, 0.0)
        else:
            p = p + jnp.where(hi, t_prev, 0.0)
            r = r + jnp.where(hi, 0.0, t_next)
    qe = (q * jnp.exp(p)).astype(BF16)
    ke = (k * jnp.exp(r)).astype(BF16)
    vb = v.astype(BF16)
    o_in = [jnp.dot(a_in[i].astype(BF16), vb[rows], preferred_element_type=F32)
            for i, rows in enumerate((top, bot))]
    o_x = jnp.dot(a_cross.astype(BF16), vb[bot if rev else top], preferred_element_type=F32)
    o_in[0 if rev else 1] = o_in[0 if rev else 1] + o_x
    o = jnp.concatenate(o_in, axis=0)
    o = o + lax.dot_general(qe, st.astype(BF16), nt, preferred_element_type=F32)
    decay = jnp.exp((p + r)[0:1, :])
    st_new = st * decay + lax.dot_general(vb, ke, (((0,), (0,)), ((), ())), preferred_element_type=F32)
    return o, st_new


def _log_sigmoid(z):
    return jnp.minimum(z, 0.0) - jnp.log(1.0 + jnp.exp(-jnp.abs(z)))


def _level_matrix_once(lvl_ref, rev):
    @pl.when(pl.program_id(0) == 0)
    def _():
        lvl_ref[...] = _level_matrix(CHUNK // 2, rev)

    return lvl_ref[...]


def _gla_kernel(q_ref, k_ref, v_ref, lr_ref, w2_ref, b2_ref, h0_ref, o_ref, fin_ref, st_ref, lvl_ref, *, rev):
    _item_init(pl.program_id(0), st_ref, h0_ref)
    lr = lr_ref[...].astype(BF16)
    lvlmat = _level_matrix_once(lvl_ref, rev)

    def head(h, carry):
        z = jnp.dot(lr, w2_ref[h], preferred_element_type=F32) + b2_ref[h]
        lg = _log_sigmoid(z) * (1.0 / GLA_TAU)
        q = q_ref[h] * (GLA_DK ** -0.5)
        o, st = _gla_chunk(q, k_ref[h], v_ref[h], lg, st_ref[h], lvlmat, rev)
        o_ref[h] = o
        st_ref[h] = st
        fin_ref[h] = st
        return carry

    lax.fori_loop(0, GLA_HEADS, head, 0)


def _gla_scan(pa, p2, plr, w2p, b2p, cache, d):
    c = CHUNK
    rev = d == 1
    hq = GLA_HEADS
    return pl.pallas_call(
        functools.partial(_gla_kernel, rev=rev),
        grid=(N_ITEMS,),
        in_specs=[
            pl.BlockSpec((hq, c, GLA_DKP), lambda w: (PA_GQ // hq, _item_rowblk(w, rev), 0)),
            pl.BlockSpec((hq, c, GLA_DKP), lambda w: (PA_GK // hq, _item_rowblk(w, rev), 0)),
            pl.BlockSpec((hq, c, GLA_DV), lambda w: (0, _item_rowblk(w, rev), 0)),
            pl.BlockSpec((None, c, MXU_W), lambda w: (0, _item_rowblk(w, rev), 0)),
            pl.BlockSpec((None, hq, MXU_W, GLA_DKP), lambda w: (d, 0, 0, 0)),
            pl.BlockSpec((None, hq, 1, GLA_DKP), lambda w: (d, 0, 0, 0)),
            pl.BlockSpec((None, None, hq, GLA_DV, GLA_DKP), lambda w: (_item_cache(w), d, 0, 0, 0)),
        ],
        out_specs=[
            pl.BlockSpec((hq, c, GLA_DV), lambda w: (0, _item_rowblk(w, rev), 0)),
            pl.BlockSpec((None, hq, GLA_DV, GLA_DKP), lambda w: (_item_fin(w), 0, 0, 0)),
        ],
        out_shape=[jax.ShapeDtypeStruct((hq, T_ALL, GLA_DV), F32),
                   jax.ShapeDtypeStruct((BATCH + 1, hq, GLA_DV, GLA_DKP), F32)],
        scratch_shapes=[pltpu.VMEM((hq, GLA_DV, GLA_DKP), F32), pltpu.VMEM((c // 2, c // 2), jnp.int32)],
        compiler_params=_cparams(("arbitrary",)),
        name="gla_scan",
    )(pa, pa, p2, plr, w2p, b2p, cache)


def _hg_pair(q2, z2, v2, lb, st_ref, pr, lvlmat, rev):
    kk = HG_EXPAND
    q2 = q2 * (HG_EXPAND ** -0.5)
    f = lb + (1.0 - lb) * _sigmoid(z2)
    lg2 = jnp.log(jnp.maximum(f, F_FLOOR))
    k2 = 1.0 - f
    outs = []
    for s in range(2):
        sl = slice(s * kk, (s + 1) * kk)
        h = 2 * pr + s
        o, st = _gla_chunk(q2[:, sl], k2[:, sl], v2[:, sl], lg2[:, sl], st_ref[h], lvlmat, rev)
        st_ref[h] = st
        outs.append(o)
    return jnp.concatenate(outs, axis=1)


def _hg_ctx_kernel(q_ref, z_ref, v_ref, lb_ref, o_ref, fin_ref, st_ref, lvl_ref, *, rev):
    st_ref[...] = jnp.zeros(st_ref.shape, st_ref.dtype)
    lvlmat = _level_matrix_once(lvl_ref, rev)

    def pair(pr, carry):
        o_ref[pr] = _hg_pair(q_ref[pr], z_ref[pr], v_ref[pr], lb_ref[pr], st_ref, pr, lvlmat, rev)
        return carry

    lax.fori_loop(0, HG_HEADS // 2, pair, 0)
    fin_ref[...] = st_ref[...]


def _hg_lat_kernel(q_ref, z_ref, v_ref, lb_ref, h0_ref, oprev_ref, o_ref, st_ref, lvl_ref, *, rev):
    del oprev_ref

    @pl.when(pl.program_id(0) % HG_LAT_ITEMS == 0)
    def _():
        st_ref[...] = h0_ref[...]

    lvlmat = _level_matrix_once(lvl_ref, rev)
    rows = DEC_SEQ // GRID_W
    ncol = CHUNK // rows
    chunks = range(HG_COLS // ncol)

    def pair(pr, carry):
        for ch in (reversed(chunks) if rev else chunks):
            cols = range(ch * ncol, (ch + 1) * ncol)
            load = lambda ref: jnp.concatenate([ref[pr, :, j, :] for j in cols], axis=0)
            o2 = _hg_pair(load(q_ref), load(z_ref), load(v_ref), lb_ref[pr], st_ref, pr, lvlmat, rev)
            for jj, j in enumerate(cols):
                o_ref[pr, :, j, :] = o2[jj * rows:(jj + 1) * rows]
        return carry

    lax.fori_loop(0, HG_HEADS // 2, pair, 0)


def _hg_scan(ph, lbp, cache, d):
    c = CHUNK
    rev = d == 1
    np_ = HG_HEADS // 2
    scratch = [pltpu.VMEM((HG_HEADS, HG_DV, HG_EXPAND), F32), pltpu.VMEM((c // 2, c // 2), jnp.int32)]
    lbspec = pl.BlockSpec((None, np_, 1, MXU_W), lambda w: (d, 0, 0, 0))

    cspec = lambda slot: pl.BlockSpec((np_, c, MXU_W), lambda w: (slot // np_, w, 0))
    o, fin = pl.pallas_call(
        functools.partial(_hg_ctx_kernel, rev=rev),
        grid=(BATCH,),
        in_specs=[cspec(PH_Q), cspec(PH_Z + d * np_), cspec(PH_I), lbspec],
        out_specs=[cspec(0), pl.BlockSpec((None, HG_HEADS, HG_DV, HG_EXPAND), lambda w: (w, 0, 0, 0))],
        out_shape=[jax.ShapeDtypeStruct((np_, T_ALL, MXU_W), F32),
                   jax.ShapeDtypeStruct((BATCH, HG_HEADS, HG_DV, HG_EXPAND), F32)],
        scratch_shapes=scratch,
        compiler_params=_cparams(("arbitrary",)),
        name="hg_scan_ctx",
    )(ph, ph, ph, lbp)

    rows = DEC_SEQ // GRID_W
    grid5 = lambda t: t.reshape(t.shape[0], T_ALL // DEC_SEQ, rows, GRID_W, MXU_W)
    seq0 = T_CTX // DEC_SEQ

    def colblk(w):
        j = w % HG_LAT_ITEMS
        return HG_LAT_ITEMS - 1 - j if rev else j

    lspec = lambda slot: pl.BlockSpec((np_, None, rows, HG_COLS, MXU_W),
                                      lambda w: (slot // np_, seq0 + w // HG_LAT_ITEMS, 0, colblk(w), 0))
    ph5 = grid5(ph)
    o5 = pl.pallas_call(
        functools.partial(_hg_lat_kernel, rev=rev),
        grid=(DEC_BATCH * HG_LAT_ITEMS,),
        in_specs=[lspec(PH_Q), lspec(PH_Z + d * np_), lspec(PH_I), lbspec,
                  pl.BlockSpec((None, None, HG_HEADS, HG_DV, HG_EXPAND),
                               lambda w: (w // HG_LAT_ITEMS, d, 0, 0, 0)),
                  pl.BlockSpec(memory_space=pl.ANY)],
        out_specs=lspec(0),
        out_shape=jax.ShapeDtypeStruct((np_, T_ALL // DEC_SEQ, rows, GRID_W, MXU_W), F32),
        input_output_aliases={5: 0},
        scratch_shapes=scratch,
        compiler_params=_cparams(("arbitrary",)),
        name="hg_scan_lat",
    )(ph5, ph5, ph5, lbp, cache, grid5(o))
    return o5.reshape(np_, T_ALL, MXU_W), fin


def _headpost_kernel(of_ref, ob_ref, gate_ref, gn_ref, out_ref, *, hp, v):
    gn = gn_ref[...]
    for s in range(hp):
        sl = slice(s * v, (s + 1) * v)
        o = of_ref[:, sl] + ob_ref[:, sl]
        y = o * lax.rsqrt(jnp.mean(o * o, axis=-1, keepdims=True) + EPS) * gn
        out_ref[:, sl] = (y * _silu(gate_ref[:, sl])).astype(BF16)


def _head_post(o_f, o_b, gate_src, gate_slot0, gain, hp, v):
    tm = TM_HEAD
    nh, _, w = o_f.shape
    return pl.pallas_call(
        functools.partial(_headpost_kernel, hp=hp, v=v),
        grid=(T_ALL // tm, nh),
        in_specs=[
            pl.BlockSpec((None, tm, w), lambda i, j: (j, i, 0)),
            pl.BlockSpec((None, tm, w), lambda i, j: (j, i, 0)),
            pl.BlockSpec((None, tm, w), lambda i, j: (gate_slot0 + j, i, 0)),
            pl.BlockSpec((1, v), lambda i, j: (0, 0)),
        ],
        out_specs=pl.BlockSpec((tm, w), lambda i, j: (i, j)),
        out_shape=jax.ShapeDtypeStruct((T_ALL, nh * w), BF16),
        compiler_params=_cparams(("parallel", "parallel")),
        name="head_post",
    )(o_f, o_b, gate_src, gain.reshape(1, v))


def _wout_kernel(a_ref, b_ref, c_ref, wa_ref, wb_ref, wc_ref, x_ref, gt_ref, o_ref):
    acc = jnp.dot(a_ref[...], wa_ref[...], preferred_element_type=F32)
    acc = acc + jnp.dot(b_ref[...], wb_ref[...], preferred_element_type=F32)
    acc = acc + jnp.dot(c_ref[...], wc_ref[...], preferred_element_type=F32)
    o_ref[...] = x_ref[...] + gt_ref[...] * acc


def _wout(ma, mb, mc, wa, wb, wc, x, mod5, layer, jgate):
    tm, tn = TM_WOUT, 1024
    return pl.pallas_call(
        _wout_kernel,
        grid=(T_ALL // tm, D_MODEL // tn),
        in_specs=[
            pl.BlockSpec((tm, S5_WIDTH), lambda i, j: (i, 0)),
            pl.BlockSpec((tm, GLA_WIDTH), lambda i, j: (i, 0)),
            pl.BlockSpec((tm, HG_WIDTH), lambda i, j: (i, 0)),
            pl.BlockSpec((S5_WIDTH, tn), lambda i, j: (0, j)),
            pl.BlockSpec((GLA_WIDTH, tn), lambda i, j: (0, j)),
            pl.BlockSpec((HG_WIDTH, tn), lambda i, j: (0, j)),
            pl.BlockSpec((tm, tn), lambda i, j: (i, j)),
            pl.BlockSpec((None, None, None, 1, tn), lambda i, j: (layer, _mod_row(i, tm), jgate, 0, j)),
        ],
        out_specs=pl.BlockSpec((tm, tn), lambda i, j: (i, j)),
        out_shape=jax.ShapeDtypeStruct((T_ALL, D_MODEL), F32),
        compiler_params=_cparams(("parallel", "arbitrary")),
        name="wout",
    )(ma, mb, mc, wa, wb, wc, x, mod5)


def _pad_heads(w, nh, dk, dkp):
    lead = w.shape[:-1]
    return jnp.pad(w.reshape(lead + (nh, dk)), ((0, 0),) * (len(lead) + 1) + ((0, dkp - dk),)).reshape(lead + (nh * dkp,))


def _split_w_in(w):
    offs = [0]
    for s in (S5_WIDTH, GLA_HEADS * GLA_DK, GLA_HEADS * GLA_DK, GLA_WIDTH, GLA_WIDTH, 2 * GLA_RANK,
              HG_WIDTH, HG_WIDTH, HG_WIDTH, HG_WIDTH, HG_WIDTH):
        offs.append(offs[-1] + s)
    (u_a, q_b, k_b, v_b, g_b, lr_b, q_c, zf_c, zb_c, i_c, g_c) = [w[..., offs[i]:offs[i + 1]] for i in range(11)]
    wa = jnp.concatenate([
        u_a,
        _pad_heads(q_b, GLA_HEADS, GLA_DK, GLA_DKP),
        _pad_heads(k_b, GLA_HEADS, GLA_DK, GLA_DKP),
    ], axis=-1)
    wlr = jnp.pad(lr_b, ((0, 0), (0, 0), (0, MXU_W - 2 * GLA_RANK)))
    wh = jnp.concatenate([q_c, zf_c, zb_c, i_c, g_c], axis=-1)
    w2 = jnp.concatenate([v_b, g_b], axis=-1)
    return wa, wlr, wh, w2


def _s5_blockdiag(bb_re, bb_im, c_re, c_im):
    eye = jnp.eye(S5_GB, dtype=F32)

    def bmat(bb):
        t = bb.reshape(2, S5_CB, S5_GB, S5_CH, S5_P)
        return jnp.einsum('dbgcp,gh->dbgchp', t, eye).reshape(2, S5_CB, S5_GB * S5_CH, S5_NB)

    def cmat(cc):
        t = cc.reshape(2, S5_CB, S5_GB, S5_CH, S5_P)
        return jnp.einsum('dbgcp,gh->dbgphc', t, eye).reshape(2, S5_CB, S5_NB, S5_GB * S5_CH)

    bbd = jnp.concatenate([bmat(bb_re), bmat(bb_im)], axis=3).astype(BF16)
    cbd = jnp.concatenate([cmat(c_re), -cmat(c_im)], axis=2).astype(BF16)
    return bbd, cbd


def _s5_state_in(re, im):
    lead = re.shape[:-2]
    f = lambda t: t.reshape(lead + (S5_CB, 1, S5_SUB, LANE))
    return jnp.concatenate([f(re), f(im)], axis=-3)


def kernel(x_prompt, x_sample, state_s5_re, state_s5_im, state_gla, state_hgrn, c, c_ctx, ada_w, ada_b, norm_g, ffn1_wg, ffn1_wu, ffn1_wd, ffn2_wg, ffn2_wu, ffn2_wd, w_in, w_out, s5_a_re, s5_a_im, s5_log_dt, s5_b_re, s5_b_im, s5_c_re, s5_c_im, s5_d, s5_glu_w, s5_glu_b, gla_w2, gla_b2, gla_norm_g, hg_lb_raw, hg_norm_g, final_norm_g):
    x = jnp.concatenate([x_prompt.reshape(T_CTX, D_MODEL), x_sample.reshape(T_LAT, D_MODEL)], axis=0)
    cond8 = jnp.zeros((N_COND, D_MODEL), F32).at[0].set(c_ctx).at[1:1 + DEC_BATCH].set(c)
    mod5 = _ada(cond8, ada_w, ada_b).reshape(DEPTH, N_COND, N_MOD, 1, D_MODEL)
    hg_lb = _hg_lb(hg_lb_raw.astype(F32).swapaxes(0, 1))

    ffn1_w = (ffn1_wg.astype(BF16), ffn1_wu.astype(BF16), ffn1_wd.astype(BF16))
    ffn2_w = (ffn2_wg.astype(BF16), ffn2_wu.astype(BF16), ffn2_wd.astype(BF16))
    wa, wlr, wh, w2 = _split_w_in(w_in.astype(BF16))

    new_re, new_im, new_gla, new_hg = [], [], [], []
    for l in range(DEPTH):
        x = _ffn(x, mod5, norm_g[l, 0], l, 0, *ffn1_w)

        h = _normmod(x, mod5, norm_g[l, 1], l, 3)
        pa = _proj(h, wa, l, MXU_W, 4)
        plr = _proj(h, wlr, l, MXU_W, 1)
        p2 = _proj(h, w2, l, GLA_DV, 2)
        ph = _proj(h, wh, l, MXU_W, 3)

        lam_re, lam_im, bb_re, bb_im = _s5_discretise(s5_a_re[l], s5_a_im[l], s5_log_dt[l], s5_b_re[l], s5_b_im[l])
        bbd, cbd = _s5_blockdiag(bb_re, bb_im, s5_c_re[l].astype(F32), s5_c_im[l].astype(F32))
        lam = _s5_state_in(lam_re, lam_im)
        s5_cache = _s5_state_in(state_s5_re[:, l], state_s5_im[:, l])
        y_f, y_b, s5_fin = _s5_scan(pa, bbd, cbd, lam, s5_cache)
        new_re.append(s5_fin[:BATCH, :, :, 0].reshape(BATCH, 2, S5_GROUPS, S5_P))
        new_im.append(s5_fin[:BATCH, :, :, 1].reshape(BATCH, 2, S5_GROUPS, S5_P))
        out_a = _s5_post(y_f, y_b, pa, s5_d[l].reshape(S5_CB, 1, MXU_W),
                         s5_glu_w[l].astype(BF16), s5_glu_b[l].reshape(1, S5_WIDTH))

        w2p = jnp.zeros((2, MXU_W, GLA_HEADS, GLA_DKP), F32)
        for d in range(2):
            w2p = w2p.at[d, d * GLA_RANK:(d + 1) * GLA_RANK, :, :GLA_DK].set(
                gla_w2[l, d].reshape(GLA_RANK, GLA_HEADS, GLA_DK))
        w2p = w2p.transpose(0, 2, 1, 3).astype(BF16)
        b2p = jnp.pad(gla_b2[l].reshape(2, GLA_HEADS, 1, GLA_DK), ((0, 0), (0, 0), (0, 0), (0, GLA_DKP - GLA_DK)))
        kpad = ((0, 0),) * 4 + ((0, GLA_DKP - GLA_DK),)
        gla_cache = jnp.pad(state_gla[:, l].swapaxes(-1, -2), kpad)
        os_, gf = [], []
        for d in range(2):
            o_d, fin_d = _gla_scan(pa, p2, plr, w2p, b2p, gla_cache, d)
            os_.append(o_d)
            gf.append(fin_d[:BATCH, ..., :GLA_DK].swapaxes(-1, -2))
        new_gla.append(jnp.stack(gf, axis=1))
        out_b = _head_post(os_[0], os_[1], p2, GLA_HEADS, gla_norm_g[l], 1, GLA_DV)

        lbp = hg_lb[l].reshape(2, HG_HEADS // 2, 1, MXU_W)
        hg_cache = state_hgrn[:, l].swapaxes(-1, -2)
        os_, hf = [], []
        for d in range(2):
            o_d, fin_d = _hg_scan(ph, lbp, hg_cache, d)
            os_.append(o_d)
            hf.append(fin_d.swapaxes(-1, -2))
        new_hg.append(jnp.stack(hf, axis=1))
        out_c = _head_post(os_[0], os_[1], ph, PH_G, hg_norm_g[l], 2, HG_DV)

        wo = w_out[l]
        x = _wout(out_a, out_b, out_c, wo[:S5_WIDTH].astype(BF16),
                  wo[S5_WIDTH:S5_WIDTH + GLA_WIDTH].astype(BF16), wo[S5_WIDTH + GLA_WIDTH:].astype(BF16),
                  x, mod5, l, 5)

        x = _ffn(x, mod5, norm_g[l, 2], l, 6, *ffn2_w)

    y_prompt = _final_norm(x, final_norm_g, 0, T_CTX).reshape(BATCH, SEQ, D_MODEL)
    y_sample = _final_norm(x, final_norm_g, T_CTX, T_LAT).reshape(DEC_BATCH, DEC_SEQ, D_MODEL)
    return (y_prompt, y_sample, jnp.stack(new_re, axis=1), jnp.stack(new_im, axis=1),
            jnp.stack(new_gla, axis=1), jnp.stack(new_hg, axis=1))
```

```python
import functools
import math

import jax
import jax.numpy as jnp
from jax import lax
from jax.experimental import pallas as pl
from jax.experimental.pallas import tpu as pltpu

F32 = jnp.float32
BF16 = jnp.bfloat16

D_MODEL = 4096
BATCH = 32
SEQ = 256
DEPTH = 2
DEC_BATCH = 2
DEC_SEQ = 4096
GRID_W = 64
S5_WIDTH = 1024
S5_CH = 16
S5_GROUPS = 64
S5_P = 64
GLA_WIDTH = 1536
GLA_HEADS = 4
GLA_DK = 192
GLA_DV = 384
GLA_RANK = 16
GLA_TAU = 16.0
HG_WIDTH = 1536
HG_EXPAND = 128
HG_HEADS = 12
HG_DV = 128
D_FF = 11008
N_MOD = 9
EPS = 1e-6
F_FLOOR = 1e-30

T_CTX = BATCH * SEQ
T_LAT = DEC_BATCH * DEC_SEQ
T_ALL = T_CTX + T_LAT
N_COND = 8
LANE = 128
S5_SUB = 8
MXU_W = 256
GLA_DKP = 256
S5_CB = 4
S5_GB = S5_GROUPS // S5_CB
S5_NB = S5_GB * S5_P
S5_NCB = 2
CHUNK = 256
HG_COLS = 8
HG_LAT_ITEMS = GRID_W // HG_COLS
LAT_CHUNKS = DEC_SEQ // CHUNK
N_ITEMS = BATCH + DEC_BATCH * LAT_CHUNKS
VMEM_LIMIT = 56 * 1024 * 1024
assert SEQ == CHUNK and T_CTX % DEC_SEQ == 0

TM_FFN = 1024
TF_FFN = 256
TM_MM = 1024
TM_WOUT = 512
TM_EW = 256
TM_HEAD = 1024
ROWS_EW = 64
COLS_DOWN = 512

PA_S5U = 0
PA_GQ = 4
PA_GK = 8
PA_SLOTS = 12
PH_Q = 0
PH_Z = 6
PH_I = 18
PH_G = 24
PH_SLOTS = 30


def _cparams(sem):
    return pltpu.CompilerParams(dimension_semantics=sem, vmem_limit_bytes=VMEM_LIMIT)


def _mod_row(i, tm):
    start = i * tm
    return jnp.where(start < T_CTX, 0, 1 + (start - T_CTX) // DEC_SEQ)


def _mod_spec(layer, j, tm, blk0=0):
    return pl.BlockSpec((None, None, None, 1, D_MODEL),
                        lambda i, *_: (layer, _mod_row(blk0 + i, tm), j, 0, 0))


def _item_rowblk(w, rev):
    j = jnp.maximum(w - BATCH, 0)
    c = j % LAT_CHUNKS
    if rev:
        c = LAT_CHUNKS - 1 - c
    return jnp.where(w < BATCH, w, BATCH + (j // LAT_CHUNKS) * LAT_CHUNKS + c)


def _item_cache(w):
    return jnp.maximum(w - BATCH, 0) // LAT_CHUNKS


def _item_fin(w):
    return jnp.minimum(w, BATCH)


def _item_init(w, st_ref, h0_ref):
    @pl.when(w < BATCH)
    def _():
        st_ref[...] = jnp.zeros(st_ref.shape, st_ref.dtype)

    @pl.when(jnp.logical_and(w >= BATCH, (w - BATCH) % LAT_CHUNKS == 0))
    def _():
        st_ref[...] = h0_ref[...]


def _sigmoid(x):
    return 1.0 / (1.0 + jnp.exp(-x))


def _silu(x):
    return x * _sigmoid(x)


def _norm_mod(x, g, sc, sh):
    ms = jnp.mean(x * x, axis=-1, keepdims=True)
    y = x * lax.rsqrt(ms + EPS) * g
    return y * (1.0 + sc) + sh


def _ada_kernel(c_ref, w_ref, b_ref, o_ref):
    s = _silu(c_ref[...]).astype(BF16)
    o_ref[...] = jnp.dot(s, w_ref[...].astype(BF16), preferred_element_type=F32) + b_ref[...]


def _ada(cond8, ada_w, ada_b):
    tn = 512
    n = N_MOD * D_MODEL
    return pl.pallas_call(
        _ada_kernel,
        grid=(DEPTH, n // tn),
        in_specs=[
            pl.BlockSpec((N_COND, D_MODEL), lambda l, j: (0, 0)),
            pl.BlockSpec((None, D_MODEL, tn), lambda l, j: (l, 0, j)),
            pl.BlockSpec((None, 1, tn), lambda l, j: (l, 0, j)),
        ],
        out_specs=pl.BlockSpec((None, N_COND, tn), lambda l, j: (l, 0, j)),
        out_shape=jax.ShapeDtypeStruct((DEPTH, N_COND, n), F32),
        compiler_params=_cparams(("parallel", "parallel")),
        name="ada",
    )(cond8, ada_w, ada_b.reshape(DEPTH, 1, n))


def _lb_kernel(raw_ref, o_ref):
    rows = [raw_ref[l] for l in range(DEPTH)]
    mx = functools.reduce(jnp.maximum, rows)
    es = [jnp.exp(r - mx) for r in rows]
    den = functools.reduce(lambda a, b: a + b, es)
    ps = [e / den for e in es]
    acc = ps[0]
    o_ref[0] = acc - ps[0]
    for l in range(1, DEPTH):
        acc = acc + ps[l]
        o_ref[l] = acc - ps[0]


def _hg_lb(raw):
    return pl.pallas_call(
        _lb_kernel,
        out_shape=jax.ShapeDtypeStruct(raw.shape, F32),
        name="hg_lb",
    )(raw)


def _ffn_kernel(x_ref, g_ref, sh_ref, sc_ref, gt_ref, wg_ref, wu_ref, wd_ref, *rest):
    o_ref, h_ref = rest[-2:]

    @pl.when(pl.program_id(1) == 0)
    def _():
        def rows(r, carry):
            sl = pl.ds(pl.multiple_of(r * ROWS_EW, ROWS_EW), ROWS_EW)
            x = x_ref[sl, :]
            h_ref[sl, :] = _norm_mod(x, g_ref[...], sc_ref[...], sh_ref[...]).astype(BF16)
            o_ref[sl, :] = x
            return carry

        lax.fori_loop(0, x_ref.shape[0] // ROWS_EW, rows, 0)

    h = h_ref[...]
    a = jnp.dot(h, wg_ref[...], preferred_element_type=F32)
    u = jnp.dot(h, wu_ref[...], preferred_element_type=F32)
    act = (_silu(a) * u).astype(BF16)
    gate = 0.5 * gt_ref[...]
    for n in range(D_MODEL // COLS_DOWN):
        sl = slice(n * COLS_DOWN, (n + 1) * COLS_DOWN)
        o_ref[:, sl] += gate[:, sl] * jnp.dot(act, wd_ref[:, sl], preferred_element_type=F32)


def _ffn(x, mod5, gain, layer, jbase, wg, wu, wd, row0=0, out=None):
    tm, tf = TM_FFN, TF_FFN
    blk0 = row0 // tm
    resident = pl.Buffered(1)
    operands = [x, gain.reshape(1, D_MODEL), mod5, mod5, mod5, wg, wu, wd]
    in_specs = [
        pl.BlockSpec((tm, D_MODEL), lambda i, f: (i, 0), pipeline_mode=resident),
        pl.BlockSpec((1, D_MODEL), lambda i, f: (0, 0)),
        _mod_spec(layer, jbase, tm, blk0),
        _mod_spec(layer, jbase + 1, tm, blk0),
        _mod_spec(layer, jbase + 2, tm, blk0),
        pl.BlockSpec((None, D_MODEL, tf), lambda i, f: (layer, 0, f)),
        pl.BlockSpec((None, D_MODEL, tf), lambda i, f: (layer, 0, f)),
        pl.BlockSpec((None, tf, D_MODEL), lambda i, f: (layer, f, 0)),
    ]
    aliases = {}
    if out is not None:
        aliases = {len(operands): 0}
        operands.append(out)
        in_specs.append(pl.BlockSpec(memory_space=pl.ANY))
    return pl.pallas_call(
        _ffn_kernel,
        grid=(x.shape[0] // tm, D_FF // tf),
        in_specs=in_specs,
        out_specs=pl.BlockSpec((tm, D_MODEL), lambda i, f: (blk0 + i, 0), pipeline_mode=resident),
        out_shape=jax.ShapeDtypeStruct((T_ALL, D_MODEL), F32),
        input_output_aliases=aliases,
        scratch_shapes=[pltpu.VMEM((tm, D_MODEL), BF16)],
        compiler_params=_cparams(("parallel", "arbitrary")),
        name="ffn",
    )(*operands)


def _normmod_kernel(x_ref, g_ref, sh_ref, sc_ref, o_ref):
    o_ref[...] = _norm_mod(x_ref[...], g_ref[...], sc_ref[...], sh_ref[...]).astype(BF16)


def _normmod(x, mod5, gain, layer, jbase):
    tm = TM_EW
    return pl.pallas_call(
        _normmod_kernel,
        grid=(T_ALL // tm,),
        in_specs=[
            pl.BlockSpec((tm, D_MODEL), lambda i: (i, 0)),
            pl.BlockSpec((1, D_MODEL), lambda i: (0, 0)),
            _mod_spec(layer, jbase, tm),
            _mod_spec(layer, jbase + 1, tm),
        ],
        out_specs=pl.BlockSpec((tm, D_MODEL), lambda i: (i, 0)),
        out_shape=jax.ShapeDtypeStruct((T_ALL, D_MODEL), BF16),
        compiler_params=_cparams(("parallel",)),
        name="normmod",
    )(x, gain.reshape(1, D_MODEL), mod5, mod5)


def _rmsnorm_kernel(x_ref, g_ref, o_ref):
    x = x_ref[...]
    ms = jnp.mean(x * x, axis=-1, keepdims=True)
    o_ref[...] = x * lax.rsqrt(ms + EPS) * g_ref[...]


def _final_norm(x, gain, row0, rows):
    tm = TM_EW
    blk0 = row0 // tm
    return pl.pallas_call(
        _rmsnorm_kernel,
        grid=(rows // tm,),
        in_specs=[pl.BlockSpec((tm, D_MODEL), lambda i: (blk0 + i, 0)),
                  pl.BlockSpec((1, D_MODEL), lambda i: (0, 0))],
        out_specs=pl.BlockSpec((tm, D_MODEL), lambda i: (i, 0)),
        out_shape=jax.ShapeDtypeStruct((rows, D_MODEL), F32),
        compiler_params=_cparams(("parallel",)),
        name="final_norm",
    )(x, gain.reshape(1, D_MODEL))


def _proj_kernel(a_ref, w_ref, o_ref, *, g, tw):
    acc = jnp.dot(a_ref[...], w_ref[...], preferred_element_type=F32)
    for s in range(g):
        o_ref[s] = acc[:, s * tw:(s + 1) * tw]


def _proj(h, w, layer, tw, g):
    n = w.shape[2]
    tm = TM_MM
    return pl.pallas_call(
        functools.partial(_proj_kernel, g=g, tw=tw),
        grid=(T_ALL // tm, n // (g * tw)),
        in_specs=[pl.BlockSpec((tm, D_MODEL), lambda i, j: (i, 0)),
                  pl.BlockSpec((None, D_MODEL, g * tw), lambda i, j: (layer, 0, j))],
        out_specs=pl.BlockSpec((g, tm, tw), lambda i, j: (j, i, 0)),
        out_shape=jax.ShapeDtypeStruct((n // tw, T_ALL, tw), F32),
        compiler_params=_cparams(("parallel", "arbitrary")),
        name="proj",
    )(h, w)


def _s5disc_kernel(are_ref, aim_ref, ldt_ref, bre_ref, bim_ref, lre_ref, lim_ref, bbre_ref, bbim_ref):
    a_re = are_ref[...]
    a_im = aim_ref[...]
    dt = jnp.exp(ldt_ref[...])
    mag = jnp.exp(a_re * dt)
    lam_re = mag * jnp.cos(a_im * dt)
    lam_im = mag * jnp.sin(a_im * dt)
    den = a_re * a_re + a_im * a_im
    z_re = ((lam_re - 1.0) * a_re + lam_im * a_im) / den
    z_im = (lam_im * a_re - (lam_re - 1.0) * a_im) / den
    b_re = bre_ref[...]
    b_im = bim_ref[...]
    lre_ref[...] = lam_re
    lim_ref[...] = lam_im
    bbre_ref[...] = z_re * b_re - z_im * b_im
    bbim_ref[...] = z_re * b_im + z_im * b_re


def _s5_discretise(a_re, a_im, log_dt, b_re, b_im):
    shp = (2, S5_GROUPS, S5_CH, S5_P)
    n = 2 * S5_GROUPS * S5_CH
    rep = lambda t: jnp.broadcast_to(t[:, :, None, :], shp).reshape(n, S5_P)
    ldt = jnp.broadcast_to(log_dt[:, :, None, None], shp).reshape(n, S5_P)
    tr = lambda t: t.swapaxes(-1, -2).reshape(n, S5_P)
    lam_re, lam_im, bb_re, bb_im = pl.pallas_call(
        _s5disc_kernel,
        out_shape=[jax.ShapeDtypeStruct((n, S5_P), F32)] * 4,
        name="s5_disc",
    )(rep(a_re), rep(a_im), ldt, tr(b_re), tr(b_im))
    return (lam_re.reshape(shp)[:, :, 0], lam_im.reshape(shp)[:, :, 0],
            bb_re.reshape(shp), bb_im.reshape(shp))


def _s5_kernel(uf_ref, ub_ref, bbd_ref, cbd_ref, lam_ref, h0_ref, yf_ref, yb_ref, fin_ref,
               st_ref, bu_ref, x_ref):
    _item_init(pl.program_id(1), st_ref, h0_ref)
    chains = [(d, j) for d in range(2) for j in range(S5_NCB)]
    u_refs = (uf_ref, ub_ref)
    y_refs = (yf_ref, yb_ref)

    sub = lambda s: pl.ds(s, CHUNK, stride=S5_SUB)
    for d, j in chains:
        bu = jnp.dot(u_refs[d][j].astype(BF16), bbd_ref[d, j], preferred_element_type=F32)
        for part in range(2):
            for s in range(S5_SUB):
                col = part * S5_NB + s * LANE
                bu_ref[d, j, part, sub(s), :] = bu[:, col:col + LANE]
    lam = [(lam_ref[d, j, 0], lam_ref[d, j, 1]) for d, j in chains]

    def step(t, carry):
        new = []
        for (d, j), (lam_re, lam_im), (xr, xi) in zip(chains, lam, carry):
            tt = CHUNK - 1 - t if d == 1 else t
            rows = pl.ds(pl.multiple_of(tt * S5_SUB, S5_SUB), S5_SUB)
            nr = lam_re * xr - lam_im * xi + bu_ref[d, j, 0, rows, :]
            ni = lam_re * xi + lam_im * xr + bu_ref[d, j, 1, rows, :]
            x_ref[d, j, 0, rows, :] = nr
            x_ref[d, j, 1, rows, :] = ni
            new.append((nr, ni))
        return tuple(new)

    init = tuple((st_ref[d, j, 0], st_ref[d, j, 1]) for d, j in chains)
    fin = lax.fori_loop(0, CHUNK, step, init, unroll=8)
    for (d, j), (xr, xi) in zip(chains, fin):
        st_ref[d, j, 0] = xr
        st_ref[d, j, 1] = xi
        fin_ref[d, j, 0] = xr
        fin_ref[d, j, 1] = xi
    for d, j in chains:
        x = jnp.concatenate([x_ref[d, j, part, sub(s), :] for part in range(2) for s in range(S5_SUB)], axis=1)
        y_refs[d][j] = jnp.dot(x.astype(BF16), cbd_ref[d, j], preferred_element_type=F32)


def _s5_scan(pa, bbd, cbd, lam, cache):
    c, nb = CHUNK, S5_NCB
    tile = (2, S5_SUB, LANE)
    uspec = lambda rev: pl.BlockSpec((nb, c, MXU_W), lambda g, w: (PA_S5U // nb + g, _item_rowblk(w, rev), 0))
    yspec = lambda rev: pl.BlockSpec((nb, c, MXU_W), lambda g, w: (g, _item_rowblk(w, rev), 0))
    return pl.pallas_call(
        _s5_kernel,
        grid=(S5_CB // nb, N_ITEMS),
        in_specs=[
            uspec(False),
            uspec(True),
            pl.BlockSpec((2, nb, MXU_W, 2 * S5_NB), lambda g, w: (0, g, 0, 0)),
            pl.BlockSpec((2, nb, 2 * S5_NB, MXU_W), lambda g, w: (0, g, 0, 0)),
            pl.BlockSpec((2, nb) + tile, lambda g, w: (0, g, 0, 0, 0)),
            pl.BlockSpec((None, 2, nb) + tile, lambda g, w: (_item_cache(w), 0, g, 0, 0, 0)),
        ],
        out_specs=[
            yspec(False),
            yspec(True),
            pl.BlockSpec((None, 2, nb) + tile, lambda g, w: (_item_fin(w), 0, g, 0, 0, 0)),
        ],
        out_shape=[jax.ShapeDtypeStruct((S5_CB, T_ALL, MXU_W), F32),
                   jax.ShapeDtypeStruct((S5_CB, T_ALL, MXU_W), F32),
                   jax.ShapeDtypeStruct((BATCH + 1, 2, S5_CB) + tile, F32)],
        scratch_shapes=[pltpu.VMEM((2, nb) + tile, F32),
                        pltpu.VMEM((2, nb, 2, c * S5_SUB, LANE), F32),
                        pltpu.VMEM((2, nb, 2, c * S5_SUB, LANE), F32)],
        compiler_params=_cparams(("parallel", "arbitrary")),
        name="s5_scan",
    )(pa, pa, bbd, cbd, lam, cache)


def _gelu_tanh(x):
    return 0.5 * x * (1.0 + jnp.tanh(math.sqrt(2.0 / math.pi) * (x + 0.044715 * x * x * x)))


def _s5post_kernel(yf_ref, yb_ref, u_ref, d_ref, w_ref, b_ref, o_ref):
    ya = []
    for cb in range(S5_CB):
        ya.append(_gelu_tanh(yf_ref[cb] + yb_ref[cb] + d_ref[cb] * u_ref[cb]))
    z = b_ref[...]
    for cb in range(S5_CB):
        z = z + jnp.dot(ya[cb].astype(BF16), w_ref[cb * MXU_W:(cb + 1) * MXU_W, :],
                        preferred_element_type=F32)
    for cb in range(S5_CB):
        sl = slice(cb * MXU_W, (cb + 1) * MXU_W)
        o_ref[:, sl] = (ya[cb] * _sigmoid(z[:, sl])).astype(BF16)


def _s5_post(yf, yb, pa, skip, glu_w, glu_b):
    tm = TM_EW
    return pl.pallas_call(
        _s5post_kernel,
        grid=(T_ALL // tm,),
        in_specs=[
            pl.BlockSpec((S5_CB, tm, MXU_W), lambda i: (0, i, 0)),
            pl.BlockSpec((S5_CB, tm, MXU_W), lambda i: (0, i, 0)),
            pl.BlockSpec((S5_CB, tm, MXU_W), lambda i: (0, i, 0)),
            pl.BlockSpec((S5_CB, 1, MXU_W), lambda i: (0, 0, 0)),
            pl.BlockSpec((S5_WIDTH, S5_WIDTH), lambda i: (0, 0)),
            pl.BlockSpec((1, S5_WIDTH), lambda i: (0, 0)),
        ],
        out_specs=pl.BlockSpec((tm, S5_WIDTH), lambda i: (i, 0)),
        out_shape=jax.ShapeDtypeStruct((T_ALL, S5_WIDTH), BF16),
        compiler_params=_cparams(("parallel",)),
        name="s5_post",
    )(yf, yb, pa, skip, glu_w, glu_b)


def _level_matrix(c, rev):
    ri = lax.broadcasted_iota(jnp.int32, (c, c), 0)
    ci = lax.broadcasted_iota(jnp.int32, (c, c), 1)
    x = ri ^ ci
    lvl = jnp.full((c, c), -1, jnp.int32)
    for l in range(int(math.log2(c))):
        lvl = jnp.where((x >> l) != 0, l, lvl)
    before = (ci >= ri) if rev else (ci <= ri)
    return jnp.where(before, lvl, 99)


def _cross_add(base, tot, m, to_second):
    out = []
    for b0 in range(0, base.shape[0], 2 * m):
        lo, hi = slice(b0, b0 + m), slice(b0 + m, b0 + 2 * m)
        if to_second:
            out += [base[lo], base[hi] + tot[lo]]
        else:
            out += [base[lo] + tot[hi], base[hi]]
    return jnp.concatenate(out, axis=0)


def _gla_chunk(q, k, v, lg, st, lvlmat, rev):
    c, kk = q.shape
    half = c // 2
    top, bot = slice(0, half), slice(half, c)
    rowk = lax.broadcasted_iota(jnp.int32, (c, kk), 0)
    nt = (((1,), (1,)), ((), ()))
    p = lg
    r = jnp.zeros_like(lg)
    diag = jnp.sum(q * k, axis=-1, keepdims=True)
    a_in = [jnp.where(lvlmat == -1, diag[rows], 0.0) for rows in (top, bot)]
    a_cross = None
    nlev = int(math.log2(c))
    for lvl in range(nlev):
        m = 1 << lvl
        qs = (q * jnp.exp(p)).astype(BF16)
        ks = (k * jnp.exp(r)).astype(BF16)
        if lvl < nlev - 1:
            for i, rows in enumerate((top, bot)):
                al = lax.dot_general(qs[rows], ks[rows], nt, preferred_element_type=F32)
                a_in[i] = jnp.where(lvlmat == lvl, al, a_in[i])
        else:
            later, earlier = (top, bot) if rev else (bot, top)
            a_cross = lax.dot_general(qs[later], ks[earlier], nt, preferred_element_type=F32)
        tot = p + r
        if m % S5_SUB == 0:
            p = _cross_add(p, tot, m, to_second=not rev)
            r = _cross_add(r, tot, m, to_second=rev)
            continue
        hi = (rowk & m) != 0
        t_prev = pltpu.roll(tot, m, 0)
        t_next = pltpu.roll(tot, c - m, 0)
        if rev:
            p = p + jnp.where(hi, 0.0, t_next)
            r = r + jnp.where(hi, t_prev, 0.0)
        else:
            p = p + jnp.where(hi, t_prev, 0.0)
            r = r + jnp.where(hi, 0.0, t_next)
    qe = (q * jnp.exp(p)).astype(BF16)
    ke = (k * jnp.exp(r)).astype(BF16)
    vb = v.astype(BF16)
    o_in = [jnp.dot(a_in[i].astype(BF16), vb[rows], preferred_element_type=F32)
            for i, rows in enumerate((top, bot))]
    o_x = jnp.dot(a_cross.astype(BF16), vb[bot if rev else top], preferred_element_type=F32)
    o_in[0 if rev else 1] = o_in[0 if rev else 1] + o_x
    o = jnp.concatenate(o_in, axis=0)
    o = o + lax.dot_general(qe, st.astype(BF16), nt, preferred_element_type=F32)
    decay = jnp.exp((p + r)[0:1, :])
    st_new = st * decay + lax.dot_general(vb, ke, (((0,), (0,)), ((), ())), preferred_element_type=F32)
    return o, st_new


def _log_sigmoid(z):
    return jnp.minimum(z, 0.0) - jnp.log(1.0 + jnp.exp(-jnp.abs(z)))


def _level_matrix_once(lvl_ref, rev):
    @pl.when(pl.program_id(0) == 0)
    def _():
        lvl_ref[...] = _level_matrix(CHUNK // 2, rev)

    return lvl_ref[...]


def _gla_kernel(q_ref, k_ref, v_ref, lr_ref, w2_ref, b2_ref, h0_ref, *rest, rev):
    o_ref, fin_ref, st_ref, lvl_ref = rest[-4:]
    w = pl.program_id(0)
    _item_init(w, st_ref, h0_ref)
    lr = lr_ref[...].astype(BF16)
    lvlmat = _level_matrix_once(lvl_ref, rev)

    def head(h, carry):
        z = jnp.dot(lr, w2_ref[h], preferred_element_type=F32) + b2_ref[h]
        lg = _log_sigmoid(z) * (1.0 / GLA_TAU)
        q = q_ref[h] * (GLA_DK ** -0.5)
        o, st = _gla_chunk(q, k_ref[h], v_ref[h], lg, st_ref[h], lvlmat, rev)
        o_ref[h] = o
        st_ref[h] = st

        @pl.when(w < BATCH)
        def _():
            fin_ref[h] = st.T[:GLA_DK]

        return carry

    lax.fori_loop(0, GLA_HEADS, head, 0)


def _gla_scan(pa, p2, plr, w2p, b2p, cache, layer, d, states):
    c = CHUNK
    rev = d == 1
    hq = GLA_HEADS
    operands = [pa, pa, p2, plr, w2p, b2p, cache]
    in_specs = [
        pl.BlockSpec((hq, c, GLA_DKP), lambda w: (PA_GQ // hq, _item_rowblk(w, rev), 0)),
        pl.BlockSpec((hq, c, GLA_DKP), lambda w: (PA_GK // hq, _item_rowblk(w, rev), 0)),
        pl.BlockSpec((hq, c, GLA_DV), lambda w: (0, _item_rowblk(w, rev), 0)),
        pl.BlockSpec((None, c, MXU_W), lambda w: (0, _item_rowblk(w, rev), 0)),
        pl.BlockSpec((None, hq, MXU_W, GLA_DKP), lambda w: (d, 0, 0, 0)),
        pl.BlockSpec((None, hq, 1, GLA_DKP), lambda w: (d, 0, 0, 0)),
        pl.BlockSpec((None, None, hq, GLA_DV, GLA_DKP), lambda w: (_item_cache(w), d, 0, 0, 0)),
    ]
    aliases = {}
    if states is not None:
        aliases = {len(operands): 1}
        operands.append(states)
        in_specs.append(pl.BlockSpec(memory_space=pl.ANY))
    return pl.pallas_call(
        functools.partial(_gla_kernel, rev=rev),
        grid=(N_ITEMS,),
        in_specs=in_specs,
        out_specs=[
            pl.BlockSpec((hq, c, GLA_DV), lambda w: (0, _item_rowblk(w, rev), 0)),
            pl.BlockSpec((None, None, None, hq, GLA_DK, GLA_DV),
                         lambda w: (jnp.minimum(w, BATCH - 1), layer, d, 0, 0, 0)),
        ],
        out_shape=[jax.ShapeDtypeStruct((hq, T_ALL, GLA_DV), F32),
                   jax.ShapeDtypeStruct((BATCH, DEPTH, 2, hq, GLA_DK, GLA_DV), F32)],
        input_output_aliases=aliases,
        scratch_shapes=[pltpu.VMEM((hq, GLA_DV, GLA_DKP), F32), pltpu.VMEM((c // 2, c // 2), jnp.int32)],
        compiler_params=_cparams(("arbitrary",)),
        name="gla_scan",
    )(*operands)


def _hg_pair(q2, z2, v2, lb, st_ref, pr, lvlmat, rev):
    kk = HG_EXPAND
    q2 = q2 * (HG_EXPAND ** -0.5)
    f = lb + (1.0 - lb) * _sigmoid(z2)
    lg2 = jnp.log(jnp.maximum(f, F_FLOOR))
    k2 = 1.0 - f
    outs = []
    for s in range(2):
        sl = slice(s * kk, (s + 1) * kk)
        h = 2 * pr + s
        o, st = _gla_chunk(q2[:, sl], k2[:, sl], v2[:, sl], lg2[:, sl], st_ref[h], lvlmat, rev)
        st_ref[h] = st
        outs.append(o)
    return jnp.concatenate(outs, axis=1)


def _hg_ctx_kernel(q_ref, z_ref, v_ref, lb_ref, *rest, rev):
    o_ref, fin_ref, st_ref, lvl_ref = rest[-4:]
    st_ref[...] = jnp.zeros(st_ref.shape, st_ref.dtype)
    lvlmat = _level_matrix_once(lvl_ref, rev)

    def pair(pr, carry):
        o_ref[pr] = _hg_pair(q_ref[pr], z_ref[pr], v_ref[pr], lb_ref[pr], st_ref, pr, lvlmat, rev)
        for s in range(2):
            fin_ref[2 * pr + s] = st_ref[2 * pr + s].T
        return carry

    lax.fori_loop(0, HG_HEADS // 2, pair, 0)


def _hg_lat_kernel(q_ref, z_ref, v_ref, lb_ref, h0_ref, oprev_ref, o_ref, st_ref, lvl_ref, *, rev):
    del oprev_ref

    @pl.when(pl.program_id(0) % HG_LAT_ITEMS == 0)
    def _():
        st_ref[...] = h0_ref[...]

    lvlmat = _level_matrix_once(lvl_ref, rev)
    rows = DEC_SEQ // GRID_W
    ncol = CHUNK // rows
    chunks = range(HG_COLS // ncol)

    def pair(pr, carry):
        for ch in (reversed(chunks) if rev else chunks):
            cols = range(ch * ncol, (ch + 1) * ncol)
            load = lambda ref: jnp.concatenate([ref[pr, :, j, :] for j in cols], axis=0)
            o2 = _hg_pair(load(q_ref), load(z_ref), load(v_ref), lb_ref[pr], st_ref, pr, lvlmat, rev)
            for jj, j in enumerate(cols):
                o_ref[pr, :, j, :] = o2[jj * rows:(jj + 1) * rows]
        return carry

    lax.fori_loop(0, HG_HEADS // 2, pair, 0)


def _hg_scan(ph, lbp, cache, layer, d, states):
    c = CHUNK
    rev = d == 1
    np_ = HG_HEADS // 2
    scratch = [pltpu.VMEM((HG_HEADS, HG_DV, HG_EXPAND), F32), pltpu.VMEM((c // 2, c // 2), jnp.int32)]
    lbspec = pl.BlockSpec((None, np_, 1, MXU_W), lambda w: (d, 0, 0, 0))

    cspec = lambda slot: pl.BlockSpec((np_, c, MXU_W), lambda w: (slot // np_, w, 0))
    operands = [ph, ph, ph, lbp]
    in_specs = [cspec(PH_Q), cspec(PH_Z + d * np_), cspec(PH_I), lbspec]
    aliases = {}
    if states is not None:
        aliases = {len(operands): 1}
        operands.append(states)
        in_specs.append(pl.BlockSpec(memory_space=pl.ANY))
    o, fin = pl.pallas_call(
        functools.partial(_hg_ctx_kernel, rev=rev),
        grid=(BATCH,),
        in_specs=in_specs,
        out_specs=[cspec(0), pl.BlockSpec((None, None, None, HG_HEADS, HG_EXPAND, HG_DV),
                                          lambda w: (w, layer, d, 0, 0, 0))],
        out_shape=[jax.ShapeDtypeStruct((np_, T_ALL, MXU_W), F32),
                   jax.ShapeDtypeStruct((BATCH, DEPTH, 2, HG_HEADS, HG_EXPAND, HG_DV), F32)],
        input_output_aliases=aliases,
        scratch_shapes=scratch,
        compiler_params=_cparams(("arbitrary",)),
        name="hg_scan_ctx",
    )(*operands)

    rows = DEC_SEQ // GRID_W
    grid5 = lambda t: t.reshape(t.shape[0], T_ALL // DEC_SEQ, rows, GRID_W, MXU_W)
    seq0 = T_CTX // DEC_SEQ

    def colblk(w):
        j = w % HG_LAT_ITEMS
        return HG_LAT_ITEMS - 1 - j if rev else j

    lspec = lambda slot: pl.BlockSpec((np_, None, rows, HG_COLS, MXU_W),
                                      lambda w: (slot // np_, seq0 + w // HG_LAT_ITEMS, 0, colblk(w), 0))
    ph5 = grid5(ph)
    o5 = pl.pallas_call(
        functools.partial(_hg_lat_kernel, rev=rev),
        grid=(DEC_BATCH * HG_LAT_ITEMS,),
        in_specs=[lspec(PH_Q), lspec(PH_Z + d * np_), lspec(PH_I), lbspec,
                  pl.BlockSpec((None, None, HG_HEADS, HG_DV, HG_EXPAND),
                               lambda w: (w // HG_LAT_ITEMS, d, 0, 0, 0)),
                  pl.BlockSpec(memory_space=pl.ANY)],
        out_specs=lspec(0),
        out_shape=jax.ShapeDtypeStruct((np_, T_ALL // DEC_SEQ, rows, GRID_W, MXU_W), F32),
        input_output_aliases={5: 0},
        scratch_shapes=scratch,
        compiler_params=_cparams(("arbitrary",)),
        name="hg_scan_lat",
    )(ph5, ph5, ph5, lbp, cache, grid5(o))
    return o5.reshape(np_, T_ALL, MXU_W), fin


def _headpost_kernel(of_ref, ob_ref, gate_ref, gn_ref, out_ref, *, hp, v):
    gn = gn_ref[...]
    for s in range(hp):
        sl = slice(s * v, (s + 1) * v)
        o = of_ref[:, sl] + ob_ref[:, sl]
        y = o * lax.rsqrt(jnp.mean(o * o, axis=-1, keepdims=True) + EPS) * gn
        out_ref[:, sl] = (y * _silu(gate_ref[:, sl])).astype(BF16)


def _head_post(o_f, o_b, gate_src, gate_slot0, gain, hp, v):
    tm = TM_HEAD
    nh, _, w = o_f.shape
    return pl.pallas_call(
        functools.partial(_headpost_kernel, hp=hp, v=v),
        grid=(T_ALL // tm, nh),
        in_specs=[
            pl.BlockSpec((None, tm, w), lambda i, j: (j, i, 0)),
            pl.BlockSpec((None, tm, w), lambda i, j: (j, i, 0)),
            pl.BlockSpec((None, tm, w), lambda i, j: (gate_slot0 + j, i, 0)),
            pl.BlockSpec((1, v), lambda i, j: (0, 0)),
        ],
        out_specs=pl.BlockSpec((tm, w), lambda i, j: (i, j)),
        out_shape=jax.ShapeDtypeStruct((T_ALL, nh * w), BF16),
        compiler_params=_cparams(("parallel", "parallel")),
        name="head_post",
    )(o_f, o_b, gate_src, gain.reshape(1, v))


def _wout_kernel(a_ref, b_ref, c_ref, wa_ref, wb_ref, wc_ref, x_ref, gt_ref, o_ref):
    acc = jnp.dot(a_ref[...], wa_ref[...], preferred_element_type=F32)
    acc = acc + jnp.dot(b_ref[...], wb_ref[...], preferred_element_type=F32)
    acc = acc + jnp.dot(c_ref[...], wc_ref[...], preferred_element_type=F32)
    o_ref[...] = x_ref[...] + gt_ref[...] * acc


def _wout(ma, mb, mc, wa, wb, wc, x, mod5, layer, jgate):
    tm, tn = TM_WOUT, 1024
    return pl.pallas_call(
        _wout_kernel,
        grid=(T_ALL // tm, D_MODEL // tn),
        in_specs=[
            pl.BlockSpec((tm, S5_WIDTH), lambda i, j: (i, 0)),
            pl.BlockSpec((tm, GLA_WIDTH), lambda i, j: (i, 0)),
            pl.BlockSpec((tm, HG_WIDTH), lambda i, j: (i, 0)),
            pl.BlockSpec((S5_WIDTH, tn), lambda i, j: (0, j)),
            pl.BlockSpec((GLA_WIDTH, tn), lambda i, j: (0, j)),
            pl.BlockSpec((HG_WIDTH, tn), lambda i, j: (0, j)),
            pl.BlockSpec((tm, tn), lambda i, j: (i, j)),
            pl.BlockSpec((None, None, None, 1, tn), lambda i, j: (layer, _mod_row(i, tm), jgate, 0, j)),
        ],
        out_specs=pl.BlockSpec((tm, tn), lambda i, j: (i, j)),
        out_shape=jax.ShapeDtypeStruct((T_ALL, D_MODEL), F32),
        compiler_params=_cparams(("parallel", "arbitrary")),
        name="wout",
    )(ma, mb, mc, wa, wb, wc, x, mod5)


def _pad_heads(w, nh, dk, dkp):
    lead = w.shape[:-1]
    return jnp.pad(w.reshape(lead + (nh, dk)), ((0, 0),) * (len(lead) + 1) + ((0, dkp - dk),)).reshape(lead + (nh * dkp,))


def _split_w_in(w):
    offs = [0]
    for s in (S5_WIDTH, GLA_HEADS * GLA_DK, GLA_HEADS * GLA_DK, GLA_WIDTH, GLA_WIDTH, 2 * GLA_RANK,
              HG_WIDTH, HG_WIDTH, HG_WIDTH, HG_WIDTH, HG_WIDTH):
        offs.append(offs[-1] + s)
    (u_a, q_b, k_b, v_b, g_b, lr_b, q_c, zf_c, zb_c, i_c, g_c) = [w[..., offs[i]:offs[i + 1]] for i in range(11)]
    wa = jnp.concatenate([
        u_a,
        _pad_heads(q_b, GLA_HEADS, GLA_DK, GLA_DKP),
        _pad_heads(k_b, GLA_HEADS, GLA_DK, GLA_DKP),
    ], axis=-1)
    wlr = jnp.pad(lr_b, ((0, 0), (0, 0), (0, MXU_W - 2 * GLA_RANK)))
    wh = jnp.concatenate([q_c, zf_c, zb_c, i_c, g_c], axis=-1)
    w2 = jnp.concatenate([v_b, g_b], axis=-1)
    return wa, wlr, wh, w2


def _s5_blockdiag(bb_re, bb_im, c_re, c_im):
    eye = jnp.eye(S5_GB, dtype=F32)

    def bmat(bb):
        t = bb.reshape(2, S5_CB, S5_GB, S5_CH, S5_P)
        return jnp.einsum('dbgcp,gh->dbgchp', t, eye).reshape(2, S5_CB, S5_GB * S5_CH, S5_NB)

    def cmat(cc):
        t = cc.reshape(2, S5_CB, S5_GB, S5_CH, S5_P)
        return jnp.einsum('dbgcp,gh->dbgphc', t, eye).reshape(2, S5_CB, S5_NB, S5_GB * S5_CH)

    bbd = jnp.concatenate([bmat(bb_re), bmat(bb_im)], axis=3).astype(BF16)
    cbd = jnp.concatenate([cmat(c_re), -cmat(c_im)], axis=2).astype(BF16)
    return bbd, cbd


def _s5_state_in(re, im):
    lead = re.shape[:-2]
    f = lambda t: t.reshape(lead + (S5_CB, 1, S5_SUB, LANE))
    return jnp.concatenate([f(re), f(im)], axis=-3)


def kernel(x_prompt, x_sample, state_s5_re, state_s5_im, state_gla, state_hgrn, c, c_ctx, ada_w, ada_b, norm_g, ffn1_wg, ffn1_wu, ffn1_wd, ffn2_wg, ffn2_wu, ffn2_wd, w_in, w_out, s5_a_re, s5_a_im, s5_log_dt, s5_b_re, s5_b_im, s5_c_re, s5_c_im, s5_d, s5_glu_w, s5_glu_b, gla_w2, gla_b2, gla_norm_g, hg_lb_raw, hg_norm_g, final_norm_g):
    x = None
    cond8 = jnp.zeros((N_COND, D_MODEL), F32).at[0].set(c_ctx).at[1:1 + DEC_BATCH].set(c)
    mod5 = _ada(cond8, ada_w, ada_b).reshape(DEPTH, N_COND, N_MOD, 1, D_MODEL)
    hg_lb = _hg_lb(hg_lb_raw.astype(F32).swapaxes(0, 1))

    ffn1_w = (ffn1_wg.astype(BF16), ffn1_wu.astype(BF16), ffn1_wd.astype(BF16))
    ffn2_w = (ffn2_wg.astype(BF16), ffn2_wu.astype(BF16), ffn2_wd.astype(BF16))
    wa, wlr, wh, w2 = _split_w_in(w_in.astype(BF16))

    new_re, new_im = [], []
    new_gla = new_hg = None
    for l in range(DEPTH):
        if l == 0:
            x = _ffn(x_prompt.reshape(T_CTX, D_MODEL), mod5, norm_g[l, 0], l, 0, *ffn1_w)
            x = _ffn(x_sample.reshape(T_LAT, D_MODEL), mod5, norm_g[l, 0], l, 0, *ffn1_w, row0=T_CTX, out=x)
        else:
            x = _ffn(x, mod5, norm_g[l, 0], l, 0, *ffn1_w)

        h = _normmod(x, mod5, norm_g[l, 1], l, 3)
        pa = _proj(h, wa, l, MXU_W, 4)
        plr = _proj(h, wlr, l, MXU_W, 1)
        p2 = _proj(h, w2, l, GLA_DV, 2)
        ph = _proj(h, wh, l, MXU_W, 3)

        lam_re, lam_im, bb_re, bb_im = _s5_discretise(s5_a_re[l], s5_a_im[l], s5_log_dt[l], s5_b_re[l], s5_b_im[l])
        bbd, cbd = _s5_blockdiag(bb_re, bb_im, s5_c_re[l].astype(F32), s5_c_im[l].astype(F32))
        lam = _s5_state_in(lam_re, lam_im)
        s5_cache = _s5_state_in(state_s5_re[:, l], state_s5_im[:, l])
        y_f, y_b, s5_fin = _s5_scan(pa, bbd, cbd, lam, s5_cache)
        new_re.append(s5_fin[:BATCH, :, :, 0].reshape(BATCH, 2, S5_GROUPS, S5_P))
        new_im.append(s5_fin[:BATCH, :, :, 1].reshape(BATCH, 2, S5_GROUPS, S5_P))
        out_a = _s5_post(y_f, y_b, pa, s5_d[l].reshape(S5_CB, 1, MXU_W),
                         s5_glu_w[l].astype(BF16), s5_glu_b[l].reshape(1, S5_WIDTH))

        w2p = jnp.zeros((2, MXU_W, GLA_HEADS, GLA_DKP), F32)
        for d in range(2):
            w2p = w2p.at[d, d * GLA_RANK:(d + 1) * GLA_RANK, :, :GLA_DK].set(
                gla_w2[l, d].reshape(GLA_RANK, GLA_HEADS, GLA_DK))
        w2p = w2p.transpose(0, 2, 1, 3).astype(BF16)
        b2p = jnp.pad(gla_b2[l].reshape(2, GLA_HEADS, 1, GLA_DK), ((0, 0), (0, 0), (0, 0), (0, GLA_DKP - GLA_DK)))
        kpad = ((0, 0),) * 4 + ((0, GLA_DKP - GLA_DK),)
        gla_cache = jnp.pad(state_gla[:, l].swapaxes(-1, -2), kpad)
        os_ = []
        for d in range(2):
            o_d, new_gla = _gla_scan(pa, p2, plr, w2p, b2p, gla_cache, l, d, new_gla)
            os_.append(o_d)
        out_b = _head_post(os_[0], os_[1], p2, GLA_HEADS, gla_norm_g[l], 1, GLA_DV)

        lbp = hg_lb[l].reshape(2, HG_HEADS // 2, 1, MXU_W)
        hg_cache = state_hgrn[:, l].swapaxes(-1, -2)
        os_ = []
        for d in range(2):
            o_d, new_hg = _hg_scan(ph, lbp, hg_cache, l, d, new_hg)
            os_.append(o_d)
        out_c = _head_post(os_[0], os_[1], ph, PH_G, hg_norm_g[l], 2, HG_DV)

        wo = w_out[l]
        x = _wout(out_a, out_b, out_c, wo[:S5_WIDTH].astype(BF16),
                  wo[S5_WIDTH:S5_WIDTH + GLA_WIDTH].astype(BF16), wo[S5_WIDTH + GLA_WIDTH:].astype(BF16),
                  x, mod5, l, 5)

        x = _ffn(x, mod5, norm_g[l, 2], l, 6, *ffn2_w)

    y_prompt = _final_norm(x, final_norm_g, 0, T_CTX).reshape(BATCH, SEQ, D_MODEL)
    y_sample = _final_norm(x, final_norm_g, T_CTX, T_LAT).reshape(DEC_BATCH, DEC_SEQ, D_MODEL)
    return (y_prompt, y_sample, jnp.stack(new_re, axis=1), jnp.stack(new_im, axis=1),
            new_gla, new_hg)
```

```python
import functools
import math

import jax
import jax.numpy as jnp
from jax import lax
from jax.experimental import pallas as pl
from jax.experimental.pallas import tpu as pltpu

F32 = jnp.float32
BF16 = jnp.bfloat16

D_MODEL = 4096
BATCH = 32
SEQ = 256
DEPTH = 2
DEC_BATCH = 2
DEC_SEQ = 4096
GRID_W = 64
S5_WIDTH = 1024
S5_CH = 16
S5_GROUPS = 64
S5_P = 64
GLA_WIDTH = 1536
GLA_HEADS = 4
GLA_DK = 192
GLA_DV = 384
GLA_RANK = 16
GLA_TAU = 16.0
HG_WIDTH = 1536
HG_EXPAND = 128
HG_HEADS = 12
HG_DV = 128
D_FF = 11008
N_MOD = 9
EPS = 1e-6
F_FLOOR = 1e-30

T_CTX = BATCH * SEQ
T_LAT = DEC_BATCH * DEC_SEQ
T_ALL = T_CTX + T_LAT
N_COND = 8
LANE = 128
S5_SUB = 8
MXU_W = 256
GLA_DKP = 256
S5_CB = 4
S5_GB = S5_GROUPS // S5_CB
S5_NB = S5_GB * S5_P
S5_NCB = 2
CHUNK = 256
HG_COLS = 8
HG_LAT_ITEMS = GRID_W // HG_COLS
LAT_CHUNKS = DEC_SEQ // CHUNK
N_ITEMS = BATCH + DEC_BATCH * LAT_CHUNKS
VMEM_LIMIT = 56 * 1024 * 1024
assert SEQ == CHUNK and T_CTX % DEC_SEQ == 0

TM_FFN = 1024
TF_FFN = 256
TM_MM = 1024
TM_WOUT = 1024
TM_EW = 256
TM_HEAD = 1024
ROWS_EW = 64
COLS_DOWN = 512

PA_S5U = 0
PA_GQ = 4
PA_GK = 8
PA_SLOTS = 12
PH_Q = 0
PH_Z = 6
PH_I = 18
PH_G = 24
PH_SLOTS = 30


def _cparams(sem):
    return pltpu.CompilerParams(dimension_semantics=sem, vmem_limit_bytes=VMEM_LIMIT)


def _mod_row(i, tm):
    start = i * tm
    return jnp.where(start < T_CTX, 0, 1 + (start - T_CTX) // DEC_SEQ)


def _mod_spec(layer, j, tm, blk0=0):
    return pl.BlockSpec((None, None, None, 1, D_MODEL),
                        lambda i, *_: (layer, _mod_row(blk0 + i, tm), j, 0, 0))


def _item_rowblk(w, rev):
    j = jnp.maximum(w - BATCH, 0)
    c = j % LAT_CHUNKS
    if rev:
        c = LAT_CHUNKS - 1 - c
    return jnp.where(w < BATCH, w, BATCH + (j // LAT_CHUNKS) * LAT_CHUNKS + c)


def _item_cache(w):
    return jnp.maximum(w - BATCH, 0) // LAT_CHUNKS


def _item_fin(w):
    return jnp.minimum(w, BATCH)


def _item_init(w, st_ref, h0_ref):
    @pl.when(w < BATCH)
    def _():
        st_ref[...] = jnp.zeros(st_ref.shape, st_ref.dtype)

    @pl.when(jnp.logical_and(w >= BATCH, (w - BATCH) % LAT_CHUNKS == 0))
    def _():
        st_ref[...] = h0_ref[...]


def _sigmoid(x):
    return 1.0 / (1.0 + jnp.exp(-x))


def _silu(x):
    return x * _sigmoid(x)


def _norm_mod(x, g, sc, sh):
    ms = jnp.mean(x * x, axis=-1, keepdims=True)
    y = x * lax.rsqrt(ms + EPS) * g
    return y * (1.0 + sc) + sh


def _ada_kernel(c_ref, w_ref, b_ref, o_ref):
    s = _silu(c_ref[...]).astype(BF16)
    o_ref[...] = jnp.dot(s, w_ref[...].astype(BF16), preferred_element_type=F32) + b_ref[...]


def _ada(cond8, ada_w, ada_b):
    tn = 512
    n = N_MOD * D_MODEL
    return pl.pallas_call(
        _ada_kernel,
        grid=(DEPTH, n // tn),
        in_specs=[
            pl.BlockSpec((N_COND, D_MODEL), lambda l, j: (0, 0)),
            pl.BlockSpec((None, D_MODEL, tn), lambda l, j: (l, 0, j)),
            pl.BlockSpec((None, 1, tn), lambda l, j: (l, 0, j)),
        ],
        out_specs=pl.BlockSpec((None, N_COND, tn), lambda l, j: (l, 0, j)),
        out_shape=jax.ShapeDtypeStruct((DEPTH, N_COND, n), F32),
        compiler_params=_cparams(("parallel", "parallel")),
        name="ada",
    )(cond8, ada_w, ada_b.reshape(DEPTH, 1, n))


def _lb_kernel(raw_ref, o_ref):
    rows = [raw_ref[l] for l in range(DEPTH)]
    mx = functools.reduce(jnp.maximum, rows)
    es = [jnp.exp(r - mx) for r in rows]
    den = functools.reduce(lambda a, b: a + b, es)
    ps = [e / den for e in es]
    acc = ps[0]
    o_ref[0] = acc - ps[0]
    for l in range(1, DEPTH):
        acc = acc + ps[l]
        o_ref[l] = acc - ps[0]


def _hg_lb(raw):
    return pl.pallas_call(
        _lb_kernel,
        out_shape=jax.ShapeDtypeStruct(raw.shape, F32),
        name="hg_lb",
    )(raw)


def _ffn_kernel(x_ref, g_ref, sh_ref, sc_ref, gt_ref, wg_ref, wu_ref, wd_ref, *rest):
    o_ref, h_ref = rest[-2:]

    @pl.when(pl.program_id(1) == 0)
    def _():
        def rows(r, carry):
            sl = pl.ds(pl.multiple_of(r * ROWS_EW, ROWS_EW), ROWS_EW)
            x = x_ref[sl, :]
            h_ref[sl, :] = _norm_mod(x, g_ref[...], sc_ref[...], sh_ref[...]).astype(BF16)
            o_ref[sl, :] = x
            return carry

        lax.fori_loop(0, x_ref.shape[0] // ROWS_EW, rows, 0)

    h = h_ref[...]
    a = jnp.dot(h, wg_ref[...], preferred_element_type=F32)
    u = jnp.dot(h, wu_ref[...], preferred_element_type=F32)
    act = (_silu(a) * u).astype(BF16)
    gate = 0.5 * gt_ref[...]
    for n in range(D_MODEL // COLS_DOWN):
        sl = slice(n * COLS_DOWN, (n + 1) * COLS_DOWN)
        o_ref[:, sl] += gate[:, sl] * jnp.dot(act, wd_ref[:, sl], preferred_element_type=F32)


def _ffn(x, mod5, gain, layer, jbase, wg, wu, wd, row0=0, out=None):
    tm, tf = TM_FFN, TF_FFN
    blk0 = row0 // tm
    resident = pl.Buffered(1)
    operands = [x, gain.reshape(1, D_MODEL), mod5, mod5, mod5, wg, wu, wd]
    in_specs = [
        pl.BlockSpec((tm, D_MODEL), lambda i, f: (i, 0), pipeline_mode=resident),
        pl.BlockSpec((1, D_MODEL), lambda i, f: (0, 0)),
        _mod_spec(layer, jbase, tm, blk0),
        _mod_spec(layer, jbase + 1, tm, blk0),
        _mod_spec(layer, jbase + 2, tm, blk0),
        pl.BlockSpec((None, D_MODEL, tf), lambda i, f: (layer, 0, f)),
        pl.BlockSpec((None, D_MODEL, tf), lambda i, f: (layer, 0, f)),
        pl.BlockSpec((None, tf, D_MODEL), lambda i, f: (layer, f, 0)),
    ]
    aliases = {}
    if out is not None:
        aliases = {len(operands): 0}
        operands.append(out)
        in_specs.append(pl.BlockSpec(memory_space=pl.ANY))
    return pl.pallas_call(
        _ffn_kernel,
        grid=(x.shape[0] // tm, D_FF // tf),
        in_specs=in_specs,
        out_specs=pl.BlockSpec((tm, D_MODEL), lambda i, f: (blk0 + i, 0), pipeline_mode=resident),
        out_shape=jax.ShapeDtypeStruct((T_ALL, D_MODEL), F32),
        input_output_aliases=aliases,
        scratch_shapes=[pltpu.VMEM((tm, D_MODEL), BF16)],
        compiler_params=_cparams(("parallel", "arbitrary")),
        name="ffn",
    )(*operands)


def _normmod_kernel(x_ref, g_ref, sh_ref, sc_ref, o_ref):
    o_ref[...] = _norm_mod(x_ref[...], g_ref[...], sc_ref[...], sh_ref[...]).astype(BF16)


def _normmod(x, mod5, gain, layer, jbase):
    tm = TM_EW
    return pl.pallas_call(
        _normmod_kernel,
        grid=(T_ALL // tm,),
        in_specs=[
            pl.BlockSpec((tm, D_MODEL), lambda i: (i, 0)),
            pl.BlockSpec((1, D_MODEL), lambda i: (0, 0)),
            _mod_spec(layer, jbase, tm),
            _mod_spec(layer, jbase + 1, tm),
        ],
        out_specs=pl.BlockSpec((tm, D_MODEL), lambda i: (i, 0)),
        out_shape=jax.ShapeDtypeStruct((T_ALL, D_MODEL), BF16),
        compiler_params=_cparams(("parallel",)),
        name="normmod",
    )(x, gain.reshape(1, D_MODEL), mod5, mod5)


def _rmsnorm_kernel(x_ref, g_ref, o_ref):
    x = x_ref[...]
    ms = jnp.mean(x * x, axis=-1, keepdims=True)
    o_ref[...] = x * lax.rsqrt(ms + EPS) * g_ref[...]


def _final_norm(x, gain, row0, rows):
    tm = TM_EW
    blk0 = row0 // tm
    return pl.pallas_call(
        _rmsnorm_kernel,
        grid=(rows // tm,),
        in_specs=[pl.BlockSpec((tm, D_MODEL), lambda i: (blk0 + i, 0)),
                  pl.BlockSpec((1, D_MODEL), lambda i: (0, 0))],
        out_specs=pl.BlockSpec((tm, D_MODEL), lambda i: (i, 0)),
        out_shape=jax.ShapeDtypeStruct((rows, D_MODEL), F32),
        compiler_params=_cparams(("parallel",)),
        name="final_norm",
    )(x, gain.reshape(1, D_MODEL))


def _proj_kernel(a_ref, w_ref, o_ref, *, g, tw):
    acc = jnp.dot(a_ref[...], w_ref[...], preferred_element_type=F32)
    for s in range(g):
        o_ref[s] = acc[:, s * tw:(s + 1) * tw]


def _proj(h, w, layer, tw, g):
    n = w.shape[2]
    tm = TM_MM
    return pl.pallas_call(
        functools.partial(_proj_kernel, g=g, tw=tw),
        grid=(T_ALL // tm, n // (g * tw)),
        in_specs=[pl.BlockSpec((tm, D_MODEL), lambda i, j: (i, 0)),
                  pl.BlockSpec((None, D_MODEL, g * tw), lambda i, j: (layer, 0, j))],
        out_specs=pl.BlockSpec((g, tm, tw), lambda i, j: (j, i, 0)),
        out_shape=jax.ShapeDtypeStruct((n // tw, T_ALL, tw), F32),
        compiler_params=_cparams(("parallel", "arbitrary")),
        name="proj",
    )(h, w)


def _s5disc_kernel(are_ref, aim_ref, ldt_ref, bre_ref, bim_ref, lre_ref, lim_ref, bbre_ref, bbim_ref):
    a_re = are_ref[...]
    a_im = aim_ref[...]
    dt = jnp.exp(ldt_ref[...])
    mag = jnp.exp(a_re * dt)
    lam_re = mag * jnp.cos(a_im * dt)
    lam_im = mag * jnp.sin(a_im * dt)
    den = a_re * a_re + a_im * a_im
    z_re = ((lam_re - 1.0) * a_re + lam_im * a_im) / den
    z_im = (lam_im * a_re - (lam_re - 1.0) * a_im) / den
    b_re = bre_ref[...]
    b_im = bim_ref[...]
    lre_ref[...] = lam_re
    lim_ref[...] = lam_im
    bbre_ref[...] = z_re * b_re - z_im * b_im
    bbim_ref[...] = z_re * b_im + z_im * b_re


def _s5_discretise(a_re, a_im, log_dt, b_re, b_im):
    shp = (2, S5_GROUPS, S5_CH, S5_P)
    n = 2 * S5_GROUPS * S5_CH
    rep = lambda t: jnp.broadcast_to(t[:, :, None, :], shp).reshape(n, S5_P)
    ldt = jnp.broadcast_to(log_dt[:, :, None, None], shp).reshape(n, S5_P)
    tr = lambda t: t.swapaxes(-1, -2).reshape(n, S5_P)
    lam_re, lam_im, bb_re, bb_im = pl.pallas_call(
        _s5disc_kernel,
        out_shape=[jax.ShapeDtypeStruct((n, S5_P), F32)] * 4,
        name="s5_disc",
    )(rep(a_re), rep(a_im), ldt, tr(b_re), tr(b_im))
    return (lam_re.reshape(shp)[:, :, 0], lam_im.reshape(shp)[:, :, 0],
            bb_re.reshape(shp), bb_im.reshape(shp))


def _s5_kernel(uf_ref, ub_ref, bbd_ref, cbd_ref, lam_ref, h0_ref, yf_ref, yb_ref, fin_ref,
               st_ref, bu_ref, x_ref):
    _item_init(pl.program_id(1), st_ref, h0_ref)
    chains = [(d, j) for d in range(2) for j in range(S5_NCB)]
    u_refs = (uf_ref, ub_ref)
    y_refs = (yf_ref, yb_ref)

    sub = lambda s: pl.ds(s, CHUNK, stride=S5_SUB)
    for d, j in chains:
        bu = jnp.dot(u_refs[d][j].astype(BF16), bbd_ref[d, j], preferred_element_type=F32)
        for part in range(2):
            for s in range(S5_SUB):
                col = part * S5_NB + s * LANE
                bu_ref[d, j, part, sub(s), :] = bu[:, col:col + LANE]
    lam = [(lam_ref[d, j, 0], lam_ref[d, j, 1]) for d, j in chains]

    def step(t, carry):
        new = []
        for (d, j), (lam_re, lam_im), (xr, xi) in zip(chains, lam, carry):
            tt = CHUNK - 1 - t if d == 1 else t
            rows = pl.ds(pl.multiple_of(tt * S5_SUB, S5_SUB), S5_SUB)
            nr = lam_re * xr - lam_im * xi + bu_ref[d, j, 0, rows, :]
            ni = lam_re * xi + lam_im * xr + bu_ref[d, j, 1, rows, :]
            x_ref[d, j, 0, rows, :] = nr
            x_ref[d, j, 1, rows, :] = ni
            new.append((nr, ni))
        return tuple(new)

    init = tuple((st_ref[d, j, 0], st_ref[d, j, 1]) for d, j in chains)
    fin = lax.fori_loop(0, CHUNK, step, init, unroll=8)
    for (d, j), (xr, xi) in zip(chains, fin):
        st_ref[d, j, 0] = xr
        st_ref[d, j, 1] = xi
        fin_ref[d, j, 0] = xr
        fin_ref[d, j, 1] = xi
    for d, j in chains:
        x = jnp.concatenate([x_ref[d, j, part, sub(s), :] for part in range(2) for s in range(S5_SUB)], axis=1)
        y_refs[d][j] = jnp.dot(x.astype(BF16), cbd_ref[d, j], preferred_element_type=F32)


def _s5_scan(pa, bbd, cbd, lam, cache):
    c, nb = CHUNK, S5_NCB
    tile = (2, S5_SUB, LANE)
    uspec = lambda rev: pl.BlockSpec((nb, c, MXU_W), lambda g, w: (PA_S5U // nb + g, _item_rowblk(w, rev), 0))
    yspec = lambda rev: pl.BlockSpec((nb, c, MXU_W), lambda g, w: (g, _item_rowblk(w, rev), 0))
    return pl.pallas_call(
        _s5_kernel,
        grid=(S5_CB // nb, N_ITEMS),
        in_specs=[
            uspec(False),
            uspec(True),
            pl.BlockSpec((2, nb, MXU_W, 2 * S5_NB), lambda g, w: (0, g, 0, 0)),
            pl.BlockSpec((2, nb, 2 * S5_NB, MXU_W), lambda g, w: (0, g, 0, 0)),
            pl.BlockSpec((2, nb) + tile, lambda g, w: (0, g, 0, 0, 0)),
            pl.BlockSpec((None, 2, nb) + tile, lambda g, w: (_item_cache(w), 0, g, 0, 0, 0)),
        ],
        out_specs=[
            yspec(False),
            yspec(True),
            pl.BlockSpec((None, 2, nb) + tile, lambda g, w: (_item_fin(w), 0, g, 0, 0, 0)),
        ],
        out_shape=[jax.ShapeDtypeStruct((S5_CB, T_ALL, MXU_W), F32),
                   jax.ShapeDtypeStruct((S5_CB, T_ALL, MXU_W), F32),
                   jax.ShapeDtypeStruct((BATCH + 1, 2, S5_CB) + tile, F32)],
        scratch_shapes=[pltpu.VMEM((2, nb) + tile, F32),
                        pltpu.VMEM((2, nb, 2, c * S5_SUB, LANE), F32),
                        pltpu.VMEM((2, nb, 2, c * S5_SUB, LANE), F32)],
        compiler_params=_cparams(("parallel", "arbitrary")),
        name="s5_scan",
    )(pa, pa, bbd, cbd, lam, cache)


def _gelu_tanh(x):
    return 0.5 * x * (1.0 + jnp.tanh(math.sqrt(2.0 / math.pi) * (x + 0.044715 * x * x * x)))


def _s5post_kernel(yf_ref, yb_ref, u_ref, d_ref, w_ref, b_ref, o_ref):
    ya = []
    for cb in range(S5_CB):
        ya.append(_gelu_tanh(yf_ref[cb] + yb_ref[cb] + d_ref[cb] * u_ref[cb]))
    z = b_ref[...]
    for cb in range(S5_CB):
        z = z + jnp.dot(ya[cb].astype(BF16), w_ref[cb * MXU_W:(cb + 1) * MXU_W, :],
                        preferred_element_type=F32)
    for cb in range(S5_CB):
        sl = slice(cb * MXU_W, (cb + 1) * MXU_W)
        o_ref[:, sl] = (ya[cb] * _sigmoid(z[:, sl])).astype(BF16)


def _s5_post(yf, yb, pa, skip, glu_w, glu_b):
    tm = TM_EW
    return pl.pallas_call(
        _s5post_kernel,
        grid=(T_ALL // tm,),
        in_specs=[
            pl.BlockSpec((S5_CB, tm, MXU_W), lambda i: (0, i, 0)),
            pl.BlockSpec((S5_CB, tm, MXU_W), lambda i: (0, i, 0)),
            pl.BlockSpec((S5_CB, tm, MXU_W), lambda i: (0, i, 0)),
            pl.BlockSpec((S5_CB, 1, MXU_W), lambda i: (0, 0, 0)),
            pl.BlockSpec((S5_WIDTH, S5_WIDTH), lambda i: (0, 0)),
            pl.BlockSpec((1, S5_WIDTH), lambda i: (0, 0)),
        ],
        out_specs=pl.BlockSpec((tm, S5_WIDTH), lambda i: (i, 0)),
        out_shape=jax.ShapeDtypeStruct((T_ALL, S5_WIDTH), BF16),
        compiler_params=_cparams(("parallel",)),
        name="s5_post",
    )(yf, yb, pa, skip, glu_w, glu_b)


def _level_matrix(c, rev):
    ri = lax.broadcasted_iota(jnp.int32, (c, c), 0)
    ci = lax.broadcasted_iota(jnp.int32, (c, c), 1)
    x = ri ^ ci
    lvl = jnp.full((c, c), -1, jnp.int32)
    for l in range(int(math.log2(c))):
        lvl = jnp.where((x >> l) != 0, l, lvl)
    before = (ci >= ri) if rev else (ci <= ri)
    return jnp.where(before, lvl, 99)


def _cross_add(base, tot, m, to_second):
    out = []
    for b0 in range(0, base.shape[0], 2 * m):
        lo, hi = slice(b0, b0 + m), slice(b0 + m, b0 + 2 * m)
        if to_second:
            out += [base[lo], base[hi] + tot[lo]]
        else:
            out += [base[lo] + tot[hi], base[hi]]
    return jnp.concatenate(out, axis=0)


def _sub_tile_levels(q, k, lg, levels, rev):
    c, kk = q.shape
    piece = 2 * S5_SUB
    row = lax.broadcasted_iota(jnp.int32, (piece, kk), 0)
    qs = [[] for _ in range(levels)]
    ks = [[] for _ in range(levels)]
    ps, rs = [], []
    for r0 in range(0, c, piece):
        rows = slice(r0, r0 + piece)
        p, r, qp, kp = lg[rows], jnp.zeros((piece, kk), F32), q[rows], k[rows]
        for lvl in range(levels):
            m = 1 << lvl
            qs[lvl].append((qp * jnp.exp(p)).astype(BF16))
            ks[lvl].append((kp * jnp.exp(r)).astype(BF16))
            hi = (row & m) != 0
            tot = p + r
            t_prev = pltpu.roll(tot, m, 0)
            t_next = pltpu.roll(tot, piece - m, 0)
            if rev:
                p, r = p + jnp.where(hi, 0.0, t_next), r + jnp.where(hi, t_prev, 0.0)
            else:
                p, r = p + jnp.where(hi, t_prev, 0.0), r + jnp.where(hi, 0.0, t_next)
        ps.append(p)
        rs.append(r)
    cat = lambda parts: jnp.concatenate(parts, axis=0)
    return [cat(x) for x in qs], [cat(x) for x in ks], cat(ps), cat(rs)


def _gla_chunk(q, k, v, lg, st, lvlmat, rev):
    c, kk = q.shape
    half = c // 2
    top, bot = slice(0, half), slice(half, c)
    nt = (((1,), (1,)), ((), ()))
    diag = jnp.sum(q * k, axis=-1, keepdims=True)
    a_in = [jnp.where(lvlmat == -1, diag[rows], 0.0) for rows in (top, bot)]
    a_cross = None
    nlev = int(math.log2(c))
    sub_levels = int(math.log2(S5_SUB))
    qs_sub, ks_sub, p, r = _sub_tile_levels(q, k, lg, sub_levels, rev)
    for lvl in range(nlev):
        m = 1 << lvl
        if lvl < sub_levels:
            qs, ks = qs_sub[lvl], ks_sub[lvl]
        else:
            qs = (q * jnp.exp(p)).astype(BF16)
            ks = (k * jnp.exp(r)).astype(BF16)
        if lvl < nlev - 1:
            for i, rows in enumerate((top, bot)):
                al = lax.dot_general(qs[rows], ks[rows], nt, preferred_element_type=F32)
                a_in[i] = jnp.where(lvlmat == lvl, al, a_in[i])
        else:
            later, earlier = (top, bot) if rev else (bot, top)
            a_cross = lax.dot_general(qs[later], ks[earlier], nt, preferred_element_type=F32)
        if lvl < sub_levels:
            continue
        tot = p + r
        p = _cross_add(p, tot, m, to_second=not rev)
        r = _cross_add(r, tot, m, to_second=rev)
    qe = (q * jnp.exp(p)).astype(BF16)
    ke = (k * jnp.exp(r)).astype(BF16)
    vb = v.astype(BF16)
    o_in = [jnp.dot(a_in[i].astype(BF16), vb[rows], preferred_element_type=F32)
            for i, rows in enumerate((top, bot))]
    o_x = jnp.dot(a_cross.astype(BF16), vb[bot if rev else top], preferred_element_type=F32)
    o_in[0 if rev else 1] = o_in[0 if rev else 1] + o_x
    o = jnp.concatenate(o_in, axis=0)
    o = o + lax.dot_general(qe, st.astype(BF16), nt, preferred_element_type=F32)
    decay = jnp.exp((p + r)[0:1, :])
    st_new = st * decay + lax.dot_general(vb, ke, (((0,), (0,)), ((), ())), preferred_element_type=F32)
    return o, st_new


def _log_sigmoid(z):
    return jnp.minimum(z, 0.0) - jnp.log(1.0 + jnp.exp(-jnp.abs(z)))


def _level_matrix_once(lvl_ref, rev):
    @pl.when(pl.program_id(0) == 0)
    def _():
        lvl_ref[...] = _level_matrix(CHUNK // 2, rev)

    return lvl_ref[...]


def _gla_kernel(q_ref, k_ref, v_ref, lr_ref, w2_ref, b2_ref, h0_ref, *rest, rev):
    o_ref, fin_ref, st_ref, lvl_ref = rest[-4:]
    w = pl.program_id(0)
    _item_init(w, st_ref, h0_ref)
    lr = lr_ref[...].astype(BF16)
    lvlmat = _level_matrix_once(lvl_ref, rev)

    def head(h, carry):
        z = jnp.dot(lr, w2_ref[h], preferred_element_type=F32) + b2_ref[h]
        lg = _log_sigmoid(z) * (1.0 / GLA_TAU)
        q = q_ref[h] * (GLA_DK ** -0.5)
        o, st = _gla_chunk(q, k_ref[h], v_ref[h], lg, st_ref[h], lvlmat, rev)
        o_ref[h] = o
        st_ref[h] = st

        @pl.when(w < BATCH)
        def _():
            fin_ref[h] = st.T[:GLA_DK]

        return carry

    lax.fori_loop(0, GLA_HEADS, head, 0)


def _gla_scan(pa, p2, plr, w2p, b2p, cache, layer, d, states):
    c = CHUNK
    rev = d == 1
    hq = GLA_HEADS
    operands = [pa, pa, p2, plr, w2p, b2p, cache]
    in_specs = [
        pl.BlockSpec((hq, c, GLA_DKP), lambda w: (PA_GQ // hq, _item_rowblk(w, rev), 0)),
        pl.BlockSpec((hq, c, GLA_DKP), lambda w: (PA_GK // hq, _item_rowblk(w, rev), 0)),
        pl.BlockSpec((hq, c, GLA_DV), lambda w: (0, _item_rowblk(w, rev), 0)),
        pl.BlockSpec((None, c, MXU_W), lambda w: (0, _item_rowblk(w, rev), 0)),
        pl.BlockSpec((None, hq, MXU_W, GLA_DKP), lambda w: (d, 0, 0, 0)),
        pl.BlockSpec((None, hq, 1, GLA_DKP), lambda w: (d, 0, 0, 0)),
        pl.BlockSpec((None, None, hq, GLA_DV, GLA_DKP), lambda w: (_item_cache(w), d, 0, 0, 0)),
    ]
    aliases = {}
    if states is not None:
        aliases = {len(operands): 1}
        operands.append(states)
        in_specs.append(pl.BlockSpec(memory_space=pl.ANY))
    return pl.pallas_call(
        functools.partial(_gla_kernel, rev=rev),
        grid=(N_ITEMS,),
        in_specs=in_specs,
        out_specs=[
            pl.BlockSpec((hq, c, GLA_DV), lambda w: (0, _item_rowblk(w, rev), 0)),
            pl.BlockSpec((None, None, None, hq, GLA_DK, GLA_DV),
                         lambda w: (jnp.minimum(w, BATCH - 1), layer, d, 0, 0, 0)),
        ],
        out_shape=[jax.ShapeDtypeStruct((hq, T_ALL, GLA_DV), F32),
                   jax.ShapeDtypeStruct((BATCH, DEPTH, 2, hq, GLA_DK, GLA_DV), F32)],
        input_output_aliases=aliases,
        scratch_shapes=[pltpu.VMEM((hq, GLA_DV, GLA_DKP), F32), pltpu.VMEM((c // 2, c // 2), jnp.int32)],
        compiler_params=_cparams(("arbitrary",)),
        name="gla_scan",
    )(*operands)


def _hg_pair(q2, z2, v2, lb, st_ref, pr, lvlmat, rev):
    kk = HG_EXPAND
    q2 = q2 * (HG_EXPAND ** -0.5)
    f = lb + (1.0 - lb) * _sigmoid(z2)
    lg2 = jnp.log(jnp.maximum(f, F_FLOOR))
    k2 = 1.0 - f
    outs = []
    for s in range(2):
        sl = slice(s * kk, (s + 1) * kk)
        h = 2 * pr + s
        o, st = _gla_chunk(q2[:, sl], k2[:, sl], v2[:, sl], lg2[:, sl], st_ref[h], lvlmat, rev)
        st_ref[h] = st
        outs.append(o)
    return jnp.concatenate(outs, axis=1)


def _hg_ctx_kernel(q_ref, z_ref, v_ref, lb_ref, *rest, rev):
    o_ref, fin_ref, st_ref, lvl_ref = rest[-4:]
    st_ref[...] = jnp.zeros(st_ref.shape, st_ref.dtype)
    lvlmat = _level_matrix_once(lvl_ref, rev)

    def pair(pr, carry):
        o_ref[pr] = _hg_pair(q_ref[pr], z_ref[pr], v_ref[pr], lb_ref[pr], st_ref, pr, lvlmat, rev)
        for s in range(2):
            fin_ref[2 * pr + s] = st_ref[2 * pr + s].T
        return carry

    lax.fori_loop(0, HG_HEADS // 2, pair, 0)


def _hg_lat_kernel(q_ref, z_ref, v_ref, lb_ref, h0_ref, oprev_ref, o_ref, st_ref, lvl_ref, *, rev):
    del oprev_ref

    @pl.when(pl.program_id(0) % HG_LAT_ITEMS == 0)
    def _():
        st_ref[...] = h0_ref[...]

    lvlmat = _level_matrix_once(lvl_ref, rev)
    rows = DEC_SEQ // GRID_W
    ncol = CHUNK // rows
    chunks = range(HG_COLS // ncol)

    def pair(pr, carry):
        for ch in (reversed(chunks) if rev else chunks):
            cols = range(ch * ncol, (ch + 1) * ncol)
            load = lambda ref: jnp.concatenate([ref[pr, :, j, :] for j in cols], axis=0)
            o2 = _hg_pair(load(q_ref), load(z_ref), load(v_ref), lb_ref[pr], st_ref, pr, lvlmat, rev)
            for jj, j in enumerate(cols):
                o_ref[pr, :, j, :] = o2[jj * rows:(jj + 1) * rows]
        return carry

    lax.fori_loop(0, HG_HEADS // 2, pair, 0)


def _hg_scan(ph, lbp, cache, layer, d, states):
    c = CHUNK
    rev = d == 1
    np_ = HG_HEADS // 2
    scratch = [pltpu.VMEM((HG_HEADS, HG_DV, HG_EXPAND), F32), pltpu.VMEM((c // 2, c // 2), jnp.int32)]
    lbspec = pl.BlockSpec((None, np_, 1, MXU_W), lambda w: (d, 0, 0, 0))

    cspec = lambda slot: pl.BlockSpec((np_, c, MXU_W), lambda w: (slot // np_, w, 0))
    operands = [ph, ph, ph, lbp]
    in_specs = [cspec(PH_Q), cspec(PH_Z + d * np_), cspec(PH_I), lbspec]
    aliases = {}
    if states is not None:
        aliases = {len(operands): 1}
        operands.append(states)
        in_specs.append(pl.BlockSpec(memory_space=pl.ANY))
    o, fin = pl.pallas_call(
        functools.partial(_hg_ctx_kernel, rev=rev),
        grid=(BATCH,),
        in_specs=in_specs,
        out_specs=[cspec(0), pl.BlockSpec((None, None, None, HG_HEADS, HG_EXPAND, HG_DV),
                                          lambda w: (w, layer, d, 0, 0, 0))],
        out_shape=[jax.ShapeDtypeStruct((np_, T_ALL, MXU_W), F32),
                   jax.ShapeDtypeStruct((BATCH, DEPTH, 2, HG_HEADS, HG_EXPAND, HG_DV), F32)],
        input_output_aliases=aliases,
        scratch_shapes=scratch,
        compiler_params=_cparams(("arbitrary",)),
        name="hg_scan_ctx",
    )(*operands)

    rows = DEC_SEQ // GRID_W
    grid5 = lambda t: t.reshape(t.shape[0], T_ALL // DEC_SEQ, rows, GRID_W, MXU_W)
    seq0 = T_CTX // DEC_SEQ

    def colblk(w):
        j = w % HG_LAT_ITEMS
        return HG_LAT_ITEMS - 1 - j if rev else j

    lspec = lambda slot: pl.BlockSpec((np_, None, rows, HG_COLS, MXU_W),
                                      lambda w: (slot // np_, seq0 + w // HG_LAT_ITEMS, 0, colblk(w), 0))
    ph5 = grid5(ph)
    o5 = pl.pallas_call(
        functools.partial(_hg_lat_kernel, rev=rev),
        grid=(DEC_BATCH * HG_LAT_ITEMS,),
        in_specs=[lspec(PH_Q), lspec(PH_Z + d * np_), lspec(PH_I), lbspec,
                  pl.BlockSpec((None, None, HG_HEADS, HG_DV, HG_EXPAND),
                               lambda w: (w // HG_LAT_ITEMS, d, 0, 0, 0)),
                  pl.BlockSpec(memory_space=pl.ANY)],
        out_specs=lspec(0),
        out_shape=jax.ShapeDtypeStruct((np_, T_ALL // DEC_SEQ, rows, GRID_W, MXU_W), F32),
        input_output_aliases={5: 0},
        scratch_shapes=scratch,
        compiler_params=_cparams(("arbitrary",)),
        name="hg_scan_lat",
    )(ph5, ph5, ph5, lbp, cache, grid5(o))
    return o5.reshape(np_, T_ALL, MXU_W), fin


def _headpost_kernel(of_ref, ob_ref, gate_ref, gn_ref, out_ref, *, hp, v):
    gn = gn_ref[...]
    for s in range(hp):
        sl = slice(s * v, (s + 1) * v)
        o = of_ref[:, sl] + ob_ref[:, sl]
        y = o * lax.rsqrt(jnp.mean(o * o, axis=-1, keepdims=True) + EPS) * gn
        out_ref[:, sl] = (y * _silu(gate_ref[:, sl])).astype(BF16)


def _head_post(o_f, o_b, gate_src, gate_slot0, gain, hp, v):
    tm = TM_HEAD
    nh, _, w = o_f.shape
    return pl.pallas_call(
        functools.partial(_headpost_kernel, hp=hp, v=v),
        grid=(T_ALL // tm, nh),
        in_specs=[
            pl.BlockSpec((None, tm, w), lambda i, j: (j, i, 0)),
            pl.BlockSpec((None, tm, w), lambda i, j: (j, i, 0)),
            pl.BlockSpec((None, tm, w), lambda i, j: (gate_slot0 + j, i, 0)),
            pl.BlockSpec((1, v), lambda i, j: (0, 0)),
        ],
        out_specs=pl.BlockSpec((tm, w), lambda i, j: (i, j)),
        out_shape=jax.ShapeDtypeStruct((T_ALL, nh * w), BF16),
        compiler_params=_cparams(("parallel", "parallel")),
        name="head_post",
    )(o_f, o_b, gate_src, gain.reshape(1, v))


def _wout_kernel(a_ref, b_ref, c_ref, wa_ref, wb_ref, wc_ref, x_ref, gt_ref, o_ref):
    acc = jnp.dot(a_ref[...], wa_ref[...], preferred_element_type=F32)
    acc = acc + jnp.dot(b_ref[...], wb_ref[...], preferred_element_type=F32)
    acc = acc + jnp.dot(c_ref[...], wc_ref[...], preferred_element_type=F32)
    o_ref[...] = x_ref[...] + gt_ref[...] * acc


def _wout(ma, mb, mc, wa, wb, wc, x, mod5, layer, jgate):
    tm, tn = TM_WOUT, 1024
    return pl.pallas_call(
        _wout_kernel,
        grid=(T_ALL // tm, D_MODEL // tn),
        in_specs=[
            pl.BlockSpec((tm, S5_WIDTH), lambda i, j: (i, 0)),
            pl.BlockSpec((tm, GLA_WIDTH), lambda i, j: (i, 0)),
            pl.BlockSpec((tm, HG_WIDTH), lambda i, j: (i, 0)),
            pl.BlockSpec((S5_WIDTH, tn), lambda i, j: (0, j)),
            pl.BlockSpec((GLA_WIDTH, tn), lambda i, j: (0, j)),
            pl.BlockSpec((HG_WIDTH, tn), lambda i, j: (0, j)),
            pl.BlockSpec((tm, tn), lambda i, j: (i, j)),
            pl.BlockSpec((None, None, None, 1, tn), lambda i, j: (layer, _mod_row(i, tm), jgate, 0, j)),
        ],
        out_specs=pl.BlockSpec((tm, tn), lambda i, j: (i, j)),
        out_shape=jax.ShapeDtypeStruct((T_ALL, D_MODEL), F32),
        compiler_params=_cparams(("parallel", "arbitrary")),
        name="wout",
    )(ma, mb, mc, wa, wb, wc, x, mod5)


def _pad_heads(w, nh, dk, dkp):
    lead = w.shape[:-1]
    return jnp.pad(w.reshape(lead + (nh, dk)), ((0, 0),) * (len(lead) + 1) + ((0, dkp - dk),)).reshape(lead + (nh * dkp,))


def _split_w_in(w):
    offs = [0]
    for s in (S5_WIDTH, GLA_HEADS * GLA_DK, GLA_HEADS * GLA_DK, GLA_WIDTH, GLA_WIDTH, 2 * GLA_RANK,
              HG_WIDTH, HG_WIDTH, HG_WIDTH, HG_WIDTH, HG_WIDTH):
        offs.append(offs[-1] + s)
    (u_a, q_b, k_b, v_b, g_b, lr_b, q_c, zf_c, zb_c, i_c, g_c) = [w[..., offs[i]:offs[i + 1]] for i in range(11)]
    wa = jnp.concatenate([
        u_a,
        _pad_heads(q_b, GLA_HEADS, GLA_DK, GLA_DKP),
        _pad_heads(k_b, GLA_HEADS, GLA_DK, GLA_DKP),
    ], axis=-1)
    wlr = jnp.pad(lr_b, ((0, 0), (0, 0), (0, MXU_W - 2 * GLA_RANK)))
    wh = jnp.concatenate([q_c, zf_c, zb_c, i_c, g_c], axis=-1)
    w2 = jnp.concatenate([v_b, g_b], axis=-1)
    return wa, wlr, wh, w2


def _s5_blockdiag(bb_re, bb_im, c_re, c_im):
    eye = jnp.eye(S5_GB, dtype=F32)

    def bmat(bb):
        t = bb.reshape(2, S5_CB, S5_GB, S5_CH, S5_P)
        return jnp.einsum('dbgcp,gh->dbgchp', t, eye).reshape(2, S5_CB, S5_GB * S5_CH, S5_NB)

    def cmat(cc):
        t = cc.reshape(2, S5_CB, S5_GB, S5_CH, S5_P)
        return jnp.einsum('dbgcp,gh->dbgphc', t, eye).reshape(2, S5_CB, S5_NB, S5_GB * S5_CH)

    bbd = jnp.concatenate([bmat(bb_re), bmat(bb_im)], axis=3).astype(BF16)
    cbd = jnp.concatenate([cmat(c_re), -cmat(c_im)], axis=2).astype(BF16)
    return bbd, cbd


def _s5_state_in(re, im):
    lead = re.shape[:-2]
    f = lambda t: t.reshape(lead + (S5_CB, 1, S5_SUB, LANE))
    return jnp.concatenate([f(re), f(im)], axis=-3)


def kernel(x_prompt, x_sample, state_s5_re, state_s5_im, state_gla, state_hgrn, c, c_ctx, ada_w, ada_b, norm_g, ffn1_wg, ffn1_wu, ffn1_wd, ffn2_wg, ffn2_wu, ffn2_wd, w_in, w_out, s5_a_re, s5_a_im, s5_log_dt, s5_b_re, s5_b_im, s5_c_re, s5_c_im, s5_d, s5_glu_w, s5_glu_b, gla_w2, gla_b2, gla_norm_g, hg_lb_raw, hg_norm_g, final_norm_g):
    x = None
    cond8 = jnp.zeros((N_COND, D_MODEL), F32).at[0].set(c_ctx).at[1:1 + DEC_BATCH].set(c)
    mod5 = _ada(cond8, ada_w, ada_b).reshape(DEPTH, N_COND, N_MOD, 1, D_MODEL)
    hg_lb = _hg_lb(hg_lb_raw.astype(F32).swapaxes(0, 1))

    ffn1_w = (ffn1_wg.astype(BF16), ffn1_wu.astype(BF16), ffn1_wd.astype(BF16))
    ffn2_w = (ffn2_wg.astype(BF16), ffn2_wu.astype(BF16), ffn2_wd.astype(BF16))
    wa, wlr, wh, w2 = _split_w_in(w_in.astype(BF16))

    new_re, new_im = [], []
    new_gla = new_hg = None
    for l in range(DEPTH):
        if l == 0:
            x = _ffn(x_prompt.reshape(T_CTX, D_MODEL), mod5, norm_g[l, 0], l, 0, *ffn1_w)
            x = _ffn(x_sample.reshape(T_LAT, D_MODEL), mod5, norm_g[l, 0], l, 0, *ffn1_w, row0=T_CTX, out=x)
        else:
            x = _ffn(x, mod5, norm_g[l, 0], l, 0, *ffn1_w)

        h = _normmod(x, mod5, norm_g[l, 1], l, 3)
        pa = _proj(h, wa, l, MXU_W, 4)
        plr = _proj(h, wlr, l, MXU_W, 1)
        p2 = _proj(h, w2, l, GLA_DV, 2)
        ph = _proj(h, wh, l, MXU_W, 3)

        lam_re, lam_im, bb_re, bb_im = _s5_discretise(s5_a_re[l], s5_a_im[l], s5_log_dt[l], s5_b_re[l], s5_b_im[l])
        bbd, cbd = _s5_blockdiag(bb_re, bb_im, s5_c_re[l].astype(F32), s5_c_im[l].astype(F32))
        lam = _s5_state_in(lam_re, lam_im)
        s5_cache = _s5_state_in(state_s5_re[:, l], state_s5_im[:, l])
        y_f, y_b, s5_fin = _s5_scan(pa, bbd, cbd, lam, s5_cache)
        new_re.append(s5_fin[:BATCH, :, :, 0].reshape(BATCH, 2, S5_GROUPS, S5_P))
        new_im.append(s5_fin[:BATCH, :, :, 1].reshape(BATCH, 2, S5_GROUPS, S5_P))
        out_a = _s5_post(y_f, y_b, pa, s5_d[l].reshape(S5_CB, 1, MXU_W),
                         s5_glu_w[l].astype(BF16), s5_glu_b[l].reshape(1, S5_WIDTH))

        w2p = jnp.zeros((2, MXU_W, GLA_HEADS, GLA_DKP), F32)
        for d in range(2):
            w2p = w2p.at[d, d * GLA_RANK:(d + 1) * GLA_RANK, :, :GLA_DK].set(
                gla_w2[l, d].reshape(GLA_RANK, GLA_HEADS, GLA_DK))
        w2p = w2p.transpose(0, 2, 1, 3).astype(BF16)
        b2p = jnp.pad(gla_b2[l].reshape(2, GLA_HEADS, 1, GLA_DK), ((0, 0), (0, 0), (0, 0), (0, GLA_DKP - GLA_DK)))
        kpad = ((0, 0),) * 4 + ((0, GLA_DKP - GLA_DK),)
        gla_cache = jnp.pad(state_gla[:, l].swapaxes(-1, -2), kpad)
        os_ = []
        for d in range(2):
            o_d, new_gla = _gla_scan(pa, p2, plr, w2p, b2p, gla_cache, l, d, new_gla)
            os_.append(o_d)
        out_b = _head_post(os_[0], os_[1], p2, GLA_HEADS, gla_norm_g[l], 1, GLA_DV)

        lbp = hg_lb[l].reshape(2, HG_HEADS // 2, 1, MXU_W)
        hg_cache = state_hgrn[:, l].swapaxes(-1, -2)
        os_ = []
        for d in range(2):
            o_d, new_hg = _hg_scan(ph, lbp, hg_cache, l, d, new_hg)
            os_.append(o_d)
        out_c = _head_post(os_[0], os_[1], ph, PH_G, hg_norm_g[l], 2, HG_DV)

        wo = w_out[l]
        x = _wout(out_a, out_b, out_c, wo[:S5_WIDTH].astype(BF16),
                  wo[S5_WIDTH:S5_WIDTH + GLA_WIDTH].astype(BF16), wo[S5_WIDTH + GLA_WIDTH:].astype(BF16),
                  x, mod5, l, 5)

        x = _ffn(x, mod5, norm_g[l, 2], l, 6, *ffn2_w)

    y_prompt = _final_norm(x, final_norm_g, 0, T_CTX).reshape(BATCH, SEQ, D_MODEL)
    y_sample = _final_norm(x, final_norm_g, T_CTX, T_LAT).reshape(DEC_BATCH, DEC_SEQ, D_MODEL)
    return (y_prompt, y_sample, jnp.stack(new_re, axis=1), jnp.stack(new_im, axis=1),
            new_gla, new_hg)
```
